```python
import jax, jax.numpy as jnp
from jax import lax
import numpy as np

D_MODEL = 1024
BATCH = 4
SEQ = 8192
DEPTH = 4

N_META = 16
ATT_BLOCK = 128
ATT_HEADS = 16
ATT_HEAD_DIM = 64
DN_HEADS = 8
DN_HEAD_DIM = 128
DN_CONV = 4
DN_CHUNK = 64
FFN_DIM = 2816
FFN_CONV = 3
N_ATTN_LAYERS = (DEPTH + 1) // 2
N_DN_LAYERS = DEPTH // 2
ATT_HD = ATT_HEADS * ATT_HEAD_DIM
DN_HD = DN_HEADS * DN_HEAD_DIM
ATT_IN = 4 * ATT_HD + ATT_HEADS
DN_IN = 4 * DN_HD + 2 * DN_HEADS
EPS = 1e-6
NEG = -1e30

kernel_name = "fox_gdn_hybrid_trunk"


def rmsnorm(x, g):
    xf = x.astype(jnp.float32)
    y = xf * lax.rsqrt(jnp.mean(xf * xf, axis=-1, keepdims=True) + EPS)
    return (y * g.astype(jnp.float32)).astype(x.dtype)


def l2norm(x):
    xf = x.astype(jnp.float32)
    return xf * lax.rsqrt(jnp.sum(xf * xf, axis=-1, keepdims=True) + EPS)


def causal_dwconv(x, w):
    K, C = w.shape
    return lax.conv_general_dilated(
        x, w[:, None, :].astype(x.dtype), window_strides=(1,), padding=[(K - 1, 0)],
        dimension_numbers=('NWC', 'WIO', 'NWC'), feature_group_count=C)


def forgetting_attention(h, w_in, b_forget, q_gain, k_gain, w_out):
    B, L, _ = h.shape
    pad = (-L) % ATT_BLOCK
    hp = jnp.pad(h, ((0, 0), (pad, 0), (0, 0)))
    Lp = L + pad
    proj = hp @ w_in
    q, k, v, og, fl = jnp.split(proj, [ATT_HD, 2 * ATT_HD, 3 * ATT_HD, 4 * ATT_HD], axis=-1)
    q = rmsnorm(q.reshape(B, Lp, ATT_HEADS, ATT_HEAD_DIM), q_gain).transpose(0, 2, 1, 3)
    k = rmsnorm(k.reshape(B, Lp, ATT_HEADS, ATT_HEAD_DIM), k_gain).transpose(0, 2, 1, 3)
    v = v.reshape(B, Lp, ATT_HEADS, ATT_HEAD_DIM).transpose(0, 2, 1, 3)
    logf = jax.nn.log_sigmoid(fl.astype(jnp.float32) + b_forget.astype(jnp.float32))
    c = jnp.cumsum(logf, axis=1).transpose(0, 2, 1)
    pos = jnp.arange(Lp)
    n_blk = Lp // ATT_BLOCK
    qb = q.reshape(B, ATT_HEADS, n_blk, ATT_BLOCK, ATT_HEAD_DIM).transpose(2, 0, 1, 3, 4)
    cb = c.reshape(B, ATT_HEADS, n_blk, ATT_BLOCK).transpose(2, 0, 1, 3)
    pb = pos.reshape(n_blk, ATT_BLOCK)
    scale = ATT_HEAD_DIM ** -0.5

    def one_block(args):
        q_i, c_i, p_i = args
        s = jnp.einsum('bhqd,bhkd->bhqk', q_i, k).astype(jnp.float32) * scale
        s = s + c_i[..., :, None] - c[:, :, None, :]
        mask = (pos[None, :] <= p_i[:, None]) & (pos[None, :] >= pad)
        p = jax.nn.softmax(jnp.where(mask, s, NEG), axis=-1)
        return jnp.einsum('bhqk,bhkd->bhqd', p.astype(v.dtype), v)

    o = lax.map(one_block, (qb, cb, pb))
    o = o.transpose(1, 0, 3, 2, 4).reshape(B, Lp, ATT_HD)
    o = o * jax.nn.sigmoid(og)
    return (o @ w_out)[:, pad:]


def gated_deltanet(h, w_in, conv_w, a_log, dt_bias, o_gain, w_out):
    B, L, _ = h.shape
    H, Dh, C = DN_HEADS, DN_HEAD_DIM, DN_CHUNK
    pad = (-L) % C
    hp = jnp.pad(h, ((0, 0), (pad, 0), (0, 0)))
    Lp = L + pad
    N = Lp // C
    proj = hp @ w_in
    qkv, og, b_logit, a_logit = jnp.split(proj, [3 * DN_HD, 4 * DN_HD, 4 * DN_HD + H], axis=-1)
    qkv = jax.nn.silu(causal_dwconv(qkv, conv_w))
    q, k, v = jnp.split(qkv, [DN_HD, 2 * DN_HD], axis=-1)
    q = l2norm(q.reshape(B, Lp, H, Dh)) * (Dh ** -0.5)
    k = l2norm(k.reshape(B, Lp, H, Dh))
    v = v.reshape(B, Lp, H, Dh).astype(jnp.float32)
    beta = jax.nn.sigmoid(b_logit.astype(jnp.float32))
    g = -jnp.exp(a_log.astype(jnp.float32)) * jax.nn.softplus(
        a_logit.astype(jnp.float32) + dt_bias.astype(jnp.float32))

    def chunk(t):
        return t.reshape(B, N, C, H, Dh).transpose(0, 3, 1, 2, 4)
    qc, kc, vc = chunk(q), chunk(k), chunk(v)
    bc = beta.reshape(B, N, C, H).transpose(0, 3, 1, 2)
    gc = jnp.cumsum(g.reshape(B, N, C, H).transpose(0, 3, 1, 2), axis=-1)
    idx = jnp.arange(C)
    strict = idx[:, None] > idx[None, :]
    incl = idx[:, None] >= idx[None, :]
    dec = jnp.exp(jnp.where(incl, gc[..., :, None] - gc[..., None, :], -jnp.inf))
    kk = jnp.einsum('bhnid,bhnjd->bhnij', kc, kc)
    A = jnp.where(strict, kk * dec * bc[..., :, None], 0.0)
    lhs = A + jnp.eye(C, dtype=jnp.float32)
    rhs = jnp.concatenate([kc * (bc * jnp.exp(gc))[..., None], vc * bc[..., None]], axis=-1)
    sol = lax.linalg.triangular_solve(lhs, rhs, left_side=True, lower=True, unit_diagonal=True)
    W, U0 = sol[..., :Dh], sol[..., Dh:]
    qk = jnp.einsum('bhnid,bhnjd->bhnij', qc, kc) * dec
    q_dec = qc * jnp.exp(gc)[..., None]
    k_dec = kc * jnp.exp(gc[..., -1:] - gc)[..., None]
    g_last = jnp.exp(gc[..., -1])

    def to_n(t):
        return jnp.moveaxis(t, 2, 0)

    def step(S, xs):
        W_n, U0_n, qk_n, qd_n, kd_n, gl_n = xs
        U = U0_n - jnp.einsum('bhcd,bhvd->bhcv', W_n, S)
        O = jnp.einsum('bhcd,bhvd->bhcv', qd_n, S) + jnp.einsum('bhij,bhjv->bhiv', qk_n, U)
        S = gl_n[..., None, None] * S + jnp.einsum('bhcv,bhcd->bhvd', U, kd_n)
        return S, O

    S0 = jnp.zeros((B, H, Dh, Dh), jnp.float32)
    _, O = lax.scan(step, S0, (to_n(W), to_n(U0), to_n(qk), to_n(q_dec), to_n(k_dec), to_n(g_last)))
    O = O.transpose(1, 0, 3, 2, 4).reshape(B, Lp, H, Dh)
    O = rmsnorm(O, o_gain) * jax.nn.silu(og.reshape(B, Lp, H, Dh).astype(jnp.float32))
    return (O.reshape(B, Lp, DN_HD).astype(h.dtype) @ w_out)[:, pad:]


def conv_ffn(h, w_up, conv_w, w_down):
    u = causal_dwconv(h @ w_up, conv_w)
    gate, up = jnp.split(u, [FFN_DIM], axis=-1)
    return (jax.nn.gelu(gate, approximate=True) * up) @ w_down


def setup_inputs(seed: int = 0) -> dict:
    key = jax.random.key(seed)
    ks = jax.random.split(key, 24)
    D = D_MODEL
    nrm = lambda k, shape, s: jax.random.normal(k, shape, jnp.float32) * s
    gain = lambda k, shape: 1.0 + 0.1 * jax.random.normal(k, shape, jnp.float32)
    dt = jnp.exp(jax.random.uniform(ks[12], (N_DN_LAYERS, DN_HEADS), jnp.float32,
                                    np.log(1e-3), np.log(1e-1)))
    return {
        "x": nrm(ks[0], (BATCH, SEQ, D), 1.0),
        "meta_tokens": nrm(ks[1], (N_META, D), 1.0),
        "norm_mix_pre": gain(ks[2], (DEPTH, D)),
        "norm_mix_post": gain(ks[3], (DEPTH, D)),
        "norm_ffn_pre": gain(ks[4], (DEPTH, D)),
        "norm_ffn_post": gain(ks[5], (DEPTH, D)),
        "attn_w_in": nrm(ks[6], (N_ATTN_LAYERS, D, ATT_IN), D ** -0.5),
        "attn_b_forget": jax.random.uniform(ks[7], (N_ATTN_LAYERS, ATT_HEADS), jnp.float32, 2.0, 6.0),
        "attn_q_norm": gain(ks[8], (N_ATTN_LAYERS, ATT_HEAD_DIM)),
        "attn_k_norm": gain(ks[9], (N_ATTN_LAYERS, ATT_HEAD_DIM)),
        "attn_w_out": nrm(ks[10], (N_ATTN_LAYERS, ATT_HD, D), ATT_HD ** -0.5),
        "dn_w_in": nrm(ks[11], (N_DN_LAYERS, D, DN_IN), D ** -0.5),
        "dn_conv": nrm(ks[13], (N_DN_LAYERS, DN_CONV, 3 * DN_HD), DN_CONV ** -0.5),
        "dn_a_log": jnp.log(jax.random.uniform(ks[14], (N_DN_LAYERS, DN_HEADS), jnp.float32, 1.0, 16.0)),
        "dn_dt_bias": dt + jnp.log(-jnp.expm1(-dt)),
        "dn_o_norm": gain(ks[15], (N_DN_LAYERS, DN_HEAD_DIM)),
        "dn_w_out": nrm(ks[16], (N_DN_LAYERS, DN_HD, D), DN_HD ** -0.5),
        "ffn_w_up": nrm(ks[17], (DEPTH, D, 2 * FFN_DIM), D ** -0.5),
        "ffn_conv": nrm(ks[18], (DEPTH, FFN_CONV, 2 * FFN_DIM), FFN_CONV ** -0.5),
        "ffn_w_down": nrm(ks[19], (DEPTH, FFN_DIM, D), FFN_DIM ** -0.5),
    }


def reference(x, meta_tokens, norm_mix_pre, norm_mix_post, norm_ffn_pre, norm_ffn_post,
              attn_w_in, attn_b_forget, attn_q_norm, attn_k_norm, attn_w_out,
              dn_w_in, dn_conv, dn_a_log, dn_dt_bias, dn_o_norm, dn_w_out,
              ffn_w_up, ffn_conv, ffn_w_down):
    B = x.shape[0]
    meta = jnp.broadcast_to(meta_tokens[None].astype(x.dtype), (B, N_META, x.shape[-1]))
    h = jnp.concatenate([meta, x], axis=1)
    for i in range(DEPTH):
        j = i // 2
        a = rmsnorm(h, norm_mix_pre[i])
        if i % 2 == 0:
            m = forgetting_attention(a, attn_w_in[j], attn_b_forget[j], attn_q_norm[j],
                                     attn_k_norm[j], attn_w_out[j])
        else:
            m = gated_deltanet(a, dn_w_in[j], dn_conv[j], dn_a_log[j], dn_dt_bias[j],
                               dn_o_norm[j], dn_w_out[j])
        h = h + rmsnorm(m, norm_mix_post[i])
        f = conv_ffn(rmsnorm(h, norm_ffn_pre[i]), ffn_w_up[i], ffn_conv[i], ffn_w_down[i])
        h = h + rmsnorm(f, norm_ffn_post[i])
    return h[:, N_META:]
```

```python
import functools
import math

import jax
import jax.numpy as jnp
from jax import lax
from jax.experimental import pallas as pl
from jax.experimental.pallas import tpu as pltpu

D_MODEL = 1024
N_META = 16
ATT_HEADS = 16
ATT_HEAD_DIM = 64
DN_HEADS = 8
DN_HEAD_DIM = 128
DN_CONV = 4
DN_CHUNK = 64
FFN_DIM = 2816
FFN_CONV = 3
EPS = 1e-6
NEG = -1e30

LANES = 128
ROW_TILE = 768
ATT_BLOCK = ROW_TILE
FFN_CHUNK = 256
PROJ_CHUNK = 512
CARRY_ROWS = 8
VMEM_LIMIT = 56 * 1024 * 1024

F32 = jnp.float32
BF16 = jnp.bfloat16


def _rmsnorm(x, g):
    return x * lax.rsqrt(jnp.mean(x * x, axis=-1, keepdims=True) + EPS) * g


def _mm(a, b):
    return jnp.dot(a, b, preferred_element_type=F32)


def _mm_nt(a, b):
    return lax.dot_general(a, b, (((1,), (1,)), ((), ())), preferred_element_type=F32)


def _mm_tn(a, b):
    return lax.dot_general(a, b, (((0,), (0,)), ((), ())), preferred_element_type=F32)


def _split3(x):
    hi = x.astype(BF16)
    r = x - hi.astype(F32)
    mid = r.astype(BF16)
    lo = (r - mid.astype(F32)).astype(BF16)
    return hi, mid, lo


def _sel_right(x, sel):
    return sum(_mm(p, sel) for p in _split3(x))


def _sel_left(sel, x):
    return sum(_mm(sel, p) for p in _split3(x))


def _softplus(x):
    return jnp.maximum(x, 0.0) + jnp.log1p(jnp.exp(-jnp.abs(x)))


def _sigmoid(x):
    return 1.0 / (1.0 + jnp.exp(-x))


def _row_in_batch(i, tiles_per_batch, rows):
    base = (i % tiles_per_batch) * rows
    return base + lax.broadcasted_iota(jnp.int32, (rows, 1), 0)


def _shifted(u, prev, s):
    rows = u.shape[0]
    r = pltpu.roll(u, s, 0)
    ridx = lax.broadcasted_iota(jnp.int32, (rows, 1), 0)
    for t in range(s):
        r = jnp.where(ridx == t, prev[CARRY_ROWS - s + t:CARRY_ROWS - s + t + 1, :], r)
    return r


def _const_spec(shape):
    nd = len(shape)
    return pl.BlockSpec(shape, lambda *_: (0,) * nd, pipeline_mode=pl.Buffered(1))


def _attn_in_kernel(h_ref, g_ref, w_ref, wft_ref, bf_ref, qg_ref, kg_ref, hsum_ref,
                    q_ref, k_ref, v_ref, og_ref, c_ref, a_scr, carry_scr, *, tiles_per_batch, pad):
    i = pl.program_id(0)
    tm = h_ref.shape[0]
    row = _row_in_batch(i, tiles_per_batch, tm)
    a = jnp.where(row >= pad, _rmsnorm(h_ref[...], g_ref[...]), 0.0)
    a_scr[...] = a.astype(BF16)

    hd = ATT_HEADS * ATT_HEAD_DIM
    n_chunks = hd // PROJ_CHUNK
    scale = ATT_HEAD_DIM ** -0.5
    for sec, (out_ref, gain_ref) in enumerate(((q_ref, qg_ref), (k_ref, kg_ref), (v_ref, None), (og_ref, None))):
        for c in range(n_chunks):
            lo = c * PROJ_CHUNK
            y = _mm(a_scr[...], w_ref[:, sec * hd + lo: sec * hd + lo + PROJ_CHUNK])
            if gain_ref is not None:
                ms = _sel_right(y * y, hsum_ref[...]) * (1.0 / ATT_HEAD_DIM)
                y = y * lax.rsqrt(ms + EPS) * gain_ref[:, lo:lo + PROJ_CHUNK]
                if sec == 0:
                    y = y * scale
            out_ref[:, lo:lo + PROJ_CHUNK] = y.astype(out_ref.dtype)

    fl = _mm_nt(wft_ref[...], a_scr[...]) + bf_ref[...]
    logf = -_softplus(-fl)
    kk = lax.broadcasted_iota(jnp.int32, (tm, tm), 0)
    jj = lax.broadcasted_iota(jnp.int32, (tm, tm), 1)
    upper = jnp.where(kk <= jj, 1.0, 0.0).astype(BF16)

    @pl.when(i % tiles_per_batch == 0)
    def _():
        carry_scr[...] = jnp.zeros_like(carry_scr)

    c = _sel_right(logf, upper) + carry_scr[...]
    carry_scr[...] = c[:, tm - 1:tm]
    for p in range(ATT_HEADS // 2):
        c_ref[0, p, 0] = c[2 * p:2 * p + 2, :]


def _attn_in(h, g, w4, wft, bfc, qg, kg, hsum, *, batch, lp, pad):
    t = h.shape[0]
    tm = ROW_TILE
    tpb = lp // tm
    hd = ATT_HEADS * ATT_HEAD_DIM
    row_spec = pl.BlockSpec((tm, D_MODEL), lambda i: (i, 0))
    out_shapes = (
        jax.ShapeDtypeStruct((t, hd), BF16),
        jax.ShapeDtypeStruct((t, hd), BF16),
        jax.ShapeDtypeStruct((t, hd), BF16),
        jax.ShapeDtypeStruct((t, hd), F32),
        jax.ShapeDtypeStruct((batch, ATT_HEADS // 2, tpb, 2, tm), F32),
    )
    return pl.pallas_call(
        functools.partial(_attn_in_kernel, tiles_per_batch=tpb, pad=pad),
        grid=(t // tm,),
        in_specs=[row_spec, _const_spec((1, D_MODEL)), _const_spec(w4.shape), _const_spec(wft.shape),
                  _const_spec(bfc.shape), _const_spec(qg.shape), _const_spec(kg.shape), _const_spec(hsum.shape)],
        out_specs=(row_spec, row_spec, row_spec, row_spec,
                   pl.BlockSpec((1, ATT_HEADS // 2, 1, 2, tm), lambda i: (i // tpb, 0, i % tpb, 0, 0))),
        out_shape=out_shapes,
        scratch_shapes=[pltpu.VMEM((tm, D_MODEL), BF16), pltpu.VMEM((ATT_HEADS, 1), F32)],
        compiler_params=pltpu.CompilerParams(dimension_semantics=("arbitrary",), vmem_limit_bytes=VMEM_LIMIT),
        name="attn_in",
    )(h, g, w4, wft, bfc, qg, kg, hsum)


def _attn_kernel(q_ref, k_ref, v_ref, c_ref, og_ref, o_ref, *, pad):
    i = pl.program_id(2)
    tq = q_ref.shape[0]
    tk = ATT_BLOCK
    half = ATT_HEAD_DIM
    lane = lax.broadcasted_iota(jnp.int32, (1, LANES), 1)
    first = lane < half
    q = q_ref[...]
    zero = jnp.zeros_like(q)
    qs = (jnp.where(first, q, zero), jnp.where(first, zero, q))
    cref = c_ref[0, 0, i][:, 0:1]
    qpos = i * tq + lax.broadcasted_iota(jnp.int32, (tq, 1), 0)

    def body(j, carry):
        ms, ls, acc = carry
        kj = k_ref[0, pl.ds(pl.multiple_of(j * tk, tk), tk), :]
        vj = v_ref[0, pl.ds(pl.multiple_of(j * tk, tk), tk), :]
        crow = c_ref[0, 0, j]
        kpos = j * tk + lax.broadcasted_iota(jnp.int32, (1, tk), 1)
        mask = (kpos <= qpos) & (kpos >= pad)
        new_ms, new_ls, pvs, alphas = [], [], [], []
        for e in range(2):
            s = _mm_nt(qs[e], kj) + (cref[e:e + 1, :] - crow[e:e + 1, :])
            s = jnp.where(mask, s, NEG)
            m_new = jnp.maximum(ms[e], jnp.max(s, axis=-1, keepdims=True))
            p = jnp.exp(s - m_new)
            alpha = jnp.exp(ms[e] - m_new)
            new_ls.append(alpha * ls[e] + jnp.sum(p, axis=-1, keepdims=True))
            new_ms.append(m_new)
            alphas.append(alpha)
            pvs.append(_mm(p.astype(BF16), vj))
        acc = jnp.where(first, alphas[0] * acc + pvs[0], alphas[1] * acc + pvs[1])
        return tuple(new_ms), tuple(new_ls), acc

    m0 = jnp.full((tq, 1), NEG, F32)
    l0 = jnp.zeros((tq, 1), F32)
    ms, ls, acc = lax.fori_loop(0, i + 1, body, ((m0, m0), (l0, l0), jnp.zeros((tq, LANES), F32)))
    o = acc / jnp.where(first, ls[0], ls[1])
    o_ref[...] = (o * _sigmoid(og_ref[...])).astype(o_ref.dtype)


def _attention(q, k, v, c, og, *, batch, lp, pad):
    t, hd = q.shape
    tq = ATT_BLOCK
    nq = lp // tq
    n_pairs = hd // LANES
    k3 = k.reshape(batch, lp, hd)
    v3 = v.reshape(batch, lp, hd)
    q_spec = pl.BlockSpec((tq, LANES), lambda b, p, i: (b * nq + i, p))
    kv_spec = pl.BlockSpec((1, lp, LANES), lambda b, p, i: (b, 0, p))
    return pl.pallas_call(
        functools.partial(_attn_kernel, pad=pad),
        grid=(batch, n_pairs, nq),
        in_specs=[q_spec, kv_spec, kv_spec,
                  pl.BlockSpec((1, 1, nq, 2, tq), lambda b, p, i: (b, p, 0, 0, 0)),
                  q_spec],
        out_specs=q_spec,
        out_shape=jax.ShapeDtypeStruct((t, hd), BF16),
        compiler_params=pltpu.CompilerParams(
            dimension_semantics=("parallel", "parallel", "arbitrary"), vmem_limit_bytes=VMEM_LIMIT),
        name="fox_attention",
    )(q, k3, v3, c, og)


def _post_ffn_kernel(h_ref, mix_ref, wo_ref, gpost_ref, gpre_ref, wup_ref, cw_ref, wdn_ref, gfpost_ref,
                     out_ref, a_scr, acc_scr, carry_scr, *, tiles_per_batch, pad):
    i = pl.program_id(0)
    tm = h_ref.shape[0]
    row = _row_in_batch(i, tiles_per_batch, tm)
    m = _mm(mix_ref[...], wo_ref[...])
    h1 = h_ref[...] + _rmsnorm(m, gpost_ref[...])
    a = jnp.where(row >= pad, _rmsnorm(h1, gpre_ref[...]), 0.0)
    a_scr[...] = a.astype(BF16)

    @pl.when(i % tiles_per_batch == 0)
    def _():
        carry_scr[...] = jnp.zeros_like(carry_scr)

    fc = FFN_CHUNK
    for c in range(FFN_DIM // fc):
        halves = []
        for base in (0, FFN_DIM):
            lo = base + c * fc
            u = _mm(a_scr[...], wup_ref[:, lo:lo + fc])
            prev = carry_scr[:, lo:lo + fc]
            y = (cw_ref[2:3, lo:lo + fc] * u
                 + cw_ref[1:2, lo:lo + fc] * _shifted(u, prev, 1)
                 + cw_ref[0:1, lo:lo + fc] * _shifted(u, prev, 2))
            carry_scr[:, lo:lo + fc] = u[tm - CARRY_ROWS:, :]
            halves.append(y)
        gate, up = halves
        gelu = 0.5 * gate * (1.0 + jnp.tanh(math.sqrt(2.0 / math.pi) * (gate + 0.044715 * (gate * gate * gate))))
        part = _mm((gelu * up).astype(BF16), wdn_ref[c * fc:(c + 1) * fc, :])
        if c == 0:
            acc_scr[...] = part
        else:
            acc_scr[...] += part
    out_ref[...] = h1 + _rmsnorm(acc_scr[...], gfpost_ref[...])


def _post_ffn(h, mix, wo, gpost, gpre, wup, cw, wdn, gfpost, *, lp, pad):
    t = h.shape[0]
    tm = ROW_TILE
    tpb = lp // tm
    row_spec = pl.BlockSpec((tm, D_MODEL), lambda i: (i, 0))
    vec = _const_spec((1, D_MODEL))
    return pl.pallas_call(
        functools.partial(_post_ffn_kernel, tiles_per_batch=tpb, pad=pad),
        grid=(t // tm,),
        in_specs=[row_spec, row_spec, _const_spec(wo.shape), vec, vec, _const_spec(wup.shape),
                  _const_spec(cw.shape), _const_spec(wdn.shape), vec],
        out_specs=row_spec,
        out_shape=jax.ShapeDtypeStruct((t, D_MODEL), F32),
        scratch_shapes=[pltpu.VMEM((tm, D_MODEL), BF16), pltpu.VMEM((tm, D_MODEL), F32),
                        pltpu.VMEM((CARRY_ROWS, 2 * FFN_DIM), F32)],
        compiler_params=pltpu.CompilerParams(dimension_semantics=("arbitrary",), vmem_limit_bytes=VMEM_LIMIT),
        name="post_ffn",
    )(h, mix, wo, gpost, gpre, wup, cw, wdn, gfpost)


def _dn_in_kernel(h_ref, g_ref, w_ref, cw_ref, wba_ref, alog_ref, dt_ref,
                  q_ref, k_ref, v_ref, og_ref, bg_ref, a_scr, carry_scr, *, tiles_per_batch, pad):
    i = pl.program_id(0)
    tm = h_ref.shape[0]
    row = _row_in_batch(i, tiles_per_batch, tm)
    a = jnp.where(row >= pad, _rmsnorm(h_ref[...], g_ref[...]), 0.0)
    a_scr[...] = a.astype(BF16)

    @pl.when(i % tiles_per_batch == 0)
    def _():
        carry_scr[...] = jnp.zeros_like(carry_scr)

    hd = DN_HEADS * DN_HEAD_DIM
    n_chunks = hd // PROJ_CHUNK
    heads_per_chunk = PROJ_CHUNK // DN_HEAD_DIM
    for sec, out_ref in enumerate((q_ref, k_ref, v_ref)):
        for c in range(n_chunks):
            lo = sec * hd + c * PROJ_CHUNK
            u = _mm(a_scr[...], w_ref[:, lo:lo + PROJ_CHUNK])
            prev = carry_scr[:, lo:lo + PROJ_CHUNK]
            y = cw_ref[DN_CONV - 1:DN_CONV, lo:lo + PROJ_CHUNK] * u
            for s in range(1, DN_CONV):
                y = y + cw_ref[DN_CONV - 1 - s:DN_CONV - s, lo:lo + PROJ_CHUNK] * _shifted(u, prev, s)
            carry_scr[:, lo:lo + PROJ_CHUNK] = u[tm - CARRY_ROWS:, :]
            y = y * _sigmoid(y)
            for hh in range(heads_per_chunk):
                yh = y[:, hh * DN_HEAD_DIM:(hh + 1) * DN_HEAD_DIM]
                if sec < 2:
                    yh = yh * lax.rsqrt(jnp.sum(yh * yh, axis=-1, keepdims=True) + EPS)
                    if sec == 0:
                        yh = yh * (DN_HEAD_DIM ** -0.5)
                col = c * PROJ_CHUNK + hh * DN_HEAD_DIM
                out_ref[:, col:col + DN_HEAD_DIM] = yh.astype(out_ref.dtype)
    for c in range(n_chunks):
        lo = 3 * hd + c * PROJ_CHUNK
        og_ref[:, c * PROJ_CHUNK:(c + 1) * PROJ_CHUNK] = _mm(a_scr[...], w_ref[:, lo:lo + PROJ_CHUNK])

    ba = _mm(a_scr[...], wba_ref[...])
    lane = lax.broadcasted_iota(jnp.int32, (1, LANES), 1)
    beta = _sigmoid(ba)
    g = -jnp.exp(alog_ref[...]) * _softplus(ba + dt_ref[...])
    bg_ref[...] = jnp.where(lane < DN_HEADS, beta, g)


def _dn_in(h, g, w4, cw, wba, alog, dtb, *, lp, pad):
    t = h.shape[0]
    tm = ROW_TILE
    tpb = lp // tm
    hd = DN_HEADS * DN_HEAD_DIM
    row_spec = pl.BlockSpec((tm, D_MODEL), lambda i: (i, 0))
    out_shapes = (
        jax.ShapeDtypeStruct((t, hd), BF16),
        jax.ShapeDtypeStruct((t, hd), BF16),
        jax.ShapeDtypeStruct((t, hd), BF16),
        jax.ShapeDtypeStruct((t, hd), F32),
        jax.ShapeDtypeStruct((t, LANES), F32),
    )
    return pl.pallas_call(
        functools.partial(_dn_in_kernel, tiles_per_batch=tpb, pad=pad),
        grid=(t // tm,),
        in_specs=[row_spec, _const_spec((1, D_MODEL)), _const_spec(w4.shape), _const_spec(cw.shape),
                  _const_spec(wba.shape), _const_spec(alog.shape), _const_spec(dtb.shape)],
        out_specs=(row_spec, row_spec, row_spec, row_spec, pl.BlockSpec((tm, LANES), lambda i: (i, 0))),
        out_shape=out_shapes,
        scratch_shapes=[pltpu.VMEM((tm, D_MODEL), BF16), pltpu.VMEM((CARRY_ROWS, 3 * hd), F32)],
        compiler_params=pltpu.CompilerParams(dimension_semantics=("arbitrary",), vmem_limit_bytes=VMEM_LIMIT),
        name="dn_in",
    )(h, g, w4, cw, wba, alog, dtb)


def _unit_lower_inverse(a_strict, blk_mask):
    n = a_strict.shape[0]
    eye = jnp.where(lax.broadcasted_iota(jnp.int32, (n, n), 0) == lax.broadcasted_iota(jnp.int32, (n, n), 1),
                    1.0, 0.0)

    def mm3(x, y):
        xh = x.astype(BF16)
        xl = (x - xh.astype(F32)).astype(BF16)
        yh = y.astype(BF16)
        yl = (y - yh.astype(F32)).astype(BF16)
        return _mm(xh, yh) + _mm(xh, yl) + _mm(xl, yh)

    d = jnp.where(blk_mask, a_strict, 0.0)
    o = a_strict - d
    mpow = -d
    t = eye + mpow
    for _ in range(3):
        mpow = mm3(mpow, mpow)
        t = mm3(t, eye + mpow)
    b = mm3(t, o)
    b2 = mm3(b, b)
    return mm3(mm3(eye - b, eye + b2), t)


def _dn_kernel(q_ref, k_ref, v_ref, bg_ref, og_ref, gain_ref, o_ref, s_scr, *, tiles_per_batch):
    i = pl.program_id(0)
    tm = q_ref.shape[0]
    cs = DN_CHUNK
    dh = DN_HEAD_DIM

    @pl.when(i % tiles_per_batch == 0)
    def _():
        s_scr[...] = jnp.zeros_like(s_scr)

    ri = lax.broadcasted_iota(jnp.int32, (cs, cs), 0)
    ci = lax.broadcasted_iota(jnp.int32, (cs, cs), 1)
    incl = ri >= ci
    strict = ri > ci
    blk = (ri // 16) == (ci // 16)
    lower = jnp.where(incl, 1.0, 0.0).astype(BF16)
    ones = jnp.ones((cs, cs), BF16)

    def chunk(cidx, _):
        r0 = pl.multiple_of(cidx * cs, cs)
        rows = pl.ds(r0, cs)
        bg = bg_ref[rows, :]
        for hh in range(DN_HEADS):
            cols = slice(hh * dh, (hh + 1) * dh)
            qh = q_ref[rows, cols].astype(F32)
            kh_b = k_ref[rows, cols]
            kh = kh_b.astype(F32)
            vh = v_ref[rows, cols].astype(F32)
            beta = bg[:, hh:hh + 1]
            g = bg[:, DN_HEADS + hh:DN_HEADS + hh + 1]
            gc = _sel_left(lower, jnp.broadcast_to(g, (cs, dh)))
            grow = _sel_left(ones, jnp.where(ri <= ci, jnp.broadcast_to(g, (cs, cs)), 0.0))
            dec = jnp.where(incl, jnp.exp(gc[:, :cs] - grow), 0.0)
            kk = _mm_nt(kh_b, kh_b)
            a_mat = jnp.where(strict, kk * dec * beta, 0.0)
            tinv = _unit_lower_inverse(a_mat, blk)
            egc = jnp.exp(gc)
            rhs = jnp.concatenate([kh * (beta * egc), vh * beta], axis=-1).astype(BF16)
            wu = _mm(tinv.astype(BF16), rhs)
            w = wu[:, :dh]
            u0 = wu[:, dh:]
            qk = jnp.where(incl, _mm_nt(q_ref[rows, cols], kh_b) * dec, 0.0)
            s_b = s_scr[hh].astype(BF16)
            u = u0 - _mm(w.astype(BF16), s_b)
            o = _mm((qh * egc).astype(BF16), s_b) + _mm(qk.astype(BF16), u.astype(BF16))
            g_last = gc[cs - 1:cs, :]
            kd = kh * jnp.exp(g_last - gc)
            s_scr[hh] = jnp.exp(g_last) * s_scr[hh] + _mm_tn(kd.astype(BF16), u.astype(BF16))
            og = og_ref[rows, cols]
            on = _rmsnorm(o, gain_ref[...]) * (og * _sigmoid(og))
            o_ref[rows, cols] = on.astype(o_ref.dtype)
        return 0

    lax.fori_loop(0, tm // cs, chunk, 0)


def _deltanet(q, k, v, bg, og, gain, *, lp):
    t, hd = q.shape
    tm = ROW_TILE
    tpb = lp // tm
    row_spec = pl.BlockSpec((tm, hd), lambda i: (i, 0))
    return pl.pallas_call(
        functools.partial(_dn_kernel, tiles_per_batch=tpb),
        grid=(t // tm,),
        in_specs=[row_spec, row_spec, row_spec, pl.BlockSpec((tm, LANES), lambda i: (i, 0)), row_spec,
                  _const_spec((1, DN_HEAD_DIM))],
        out_specs=row_spec,
        out_shape=jax.ShapeDtypeStruct((t, hd), BF16),
        scratch_shapes=[pltpu.VMEM((DN_HEADS, DN_HEAD_DIM, DN_HEAD_DIM), F32)],
        compiler_params=pltpu.CompilerParams(dimension_semantics=("arbitrary",), vmem_limit_bytes=VMEM_LIMIT),
        name="gated_deltanet",
    )(q, k, v, bg, og, gain)


def _pad_cols(w, n):
    return jnp.pad(w, ((0, 0), (0, n - w.shape[1])))


def kernel(x, meta_tokens, norm_mix_pre, norm_mix_post, norm_ffn_pre, norm_ffn_post, attn_w_in, attn_b_forget, attn_q_norm, attn_k_norm, attn_w_out, dn_w_in, dn_conv, dn_a_log, dn_dt_bias, dn_o_norm, dn_w_out, ffn_w_up, ffn_conv, ffn_w_down):
    batch, seq, d = x.shape
    assert d == D_MODEL
    depth = norm_mix_pre.shape[0]
    length = N_META + seq
    lp = -(-length // ROW_TILE) * ROW_TILE
    pad = lp - length
    assert pad % DN_CHUNK == (-length) % DN_CHUNK

    meta = jnp.broadcast_to(meta_tokens[None].astype(x.dtype), (batch, N_META, d))
    h = jnp.concatenate([jnp.zeros((batch, pad, d), x.dtype), meta, x], axis=1).reshape(batch * lp, d)

    att_hd = ATT_HEADS * ATT_HEAD_DIM
    dn_hd = DN_HEADS * DN_HEAD_DIM
    r = lax.broadcasted_iota(jnp.int32, (PROJ_CHUNK, PROJ_CHUNK), 0) // ATT_HEAD_DIM
    c = lax.broadcasted_iota(jnp.int32, (PROJ_CHUNK, PROJ_CHUNK), 1) // ATT_HEAD_DIM
    head_sum = (r == c).astype(BF16)
    vec = lambda g: g.reshape(1, -1).astype(F32)

    for i in range(depth):
        j = i // 2
        if i % 2 == 0:
            w_in = attn_w_in[j]
            q, k, v, og, cum = _attn_in(
                h, vec(norm_mix_pre[i]), w_in[:, :4 * att_hd].astype(BF16),
                w_in[:, 4 * att_hd:].T.astype(BF16), attn_b_forget[j].reshape(ATT_HEADS, 1).astype(F32),
                vec(jnp.tile(attn_q_norm[j], ATT_HEADS)), vec(jnp.tile(attn_k_norm[j], ATT_HEADS)), head_sum,
                batch=batch, lp=lp, pad=pad)
            mix = _attention(q, k, v, cum, og, batch=batch, lp=lp, pad=pad)
            w_out = attn_w_out[j]
        else:
            w_in = dn_w_in[j]
            zeros = jnp.zeros((DN_HEADS,), F32)
            q, k, v, og, bg = _dn_in(
                h, vec(norm_mix_pre[i]), w_in[:, :4 * dn_hd].astype(BF16), dn_conv[j].astype(F32),
                _pad_cols(w_in[:, 4 * dn_hd:], LANES).astype(BF16),
                _pad_cols(vec(jnp.concatenate([zeros, dn_a_log[j]])), LANES),
                _pad_cols(vec(jnp.concatenate([zeros, dn_dt_bias[j]])), LANES),
                lp=lp, pad=pad)
            mix = _deltanet(q, k, v, bg, og, vec(dn_o_norm[j]), lp=lp)
            w_out = dn_w_out[j]
        h = _post_ffn(h, mix, w_out.astype(BF16), vec(norm_mix_post[i]), vec(norm_ffn_pre[i]),
                      ffn_w_up[i].astype(BF16), ffn_conv[i].astype(F32), ffn_w_down[i].astype(BF16),
                      vec(norm_ffn_post[i]), lp=lp, pad=pad)
    return h.reshape(batch, lp, d)[:, pad + N_META:]
```

```python
import functools
import math

import jax
import jax.numpy as jnp
from jax import lax
from jax.experimental import pallas as pl
from jax.experimental.pallas import tpu as pltpu

D_MODEL = 1024
N_META = 16
ATT_HEADS = 16
ATT_HEAD_DIM = 64
DN_HEADS = 8
DN_HEAD_DIM = 128
DN_CONV = 4
DN_CHUNK = 128
DN_BLOCK = 16
FFN_DIM = 2816
FFN_CONV = 3
EPS = 1e-6
NEG = -1e30
LOG2E = 1.4426950408889634

LANES = 128
ROW_TILE = 768
ATT_BLOCK = ROW_TILE
FFN_CHUNK = 256
PROJ_CHUNK = 512
CARRY_ROWS = 8
ELEM_ROWS = 64
SPLIT = 3
VMEM_LIMIT = 56 * 1024 * 1024

F32 = jnp.float32
BF16 = jnp.bfloat16


def _rmsnorm(x, g):
    return x * lax.rsqrt(jnp.mean(x * x, axis=-1, keepdims=True) + EPS) * g


def _mm(a, b):
    return jnp.dot(a, b, preferred_element_type=F32)


def _mm_nt(a, b):
    return lax.dot_general(a, b, (((1,), (1,)), ((), ())), preferred_element_type=F32)


def _mm_tn(a, b):
    return lax.dot_general(a, b, (((0,), (0,)), ((), ())), preferred_element_type=F32)


def _bmm(a, b):
    return _mm(a.astype(BF16), b.astype(BF16))


def _split3(x):
    hi = x.astype(BF16)
    r = x - hi.astype(F32)
    mid = r.astype(BF16)
    lo = (r - mid.astype(F32)).astype(BF16)
    return hi, mid, lo


def _sel_right(x, sel):
    return sum(_mm(p, sel) for p in _split3(x))


def _sel_right_t(x, sel):
    return sum(_mm_nt(p, sel) for p in _split3(x))


def _sel_left(sel, x):
    return sum(_mm(sel, p) for p in _split3(x))


def _softplus(x):
    return jnp.maximum(x, 0.0) + jnp.log1p(jnp.exp(-jnp.abs(x)))


def _sigmoid(x):
    return 1.0 / (1.0 + jnp.exp(-x))


def _row_in_batch(i, tiles_per_batch, rows):
    base = (i % tiles_per_batch) * rows
    return base + lax.broadcasted_iota(jnp.int32, (rows, 1), 0)


def _causal_conv(window, taps):
    kw = len(taps)
    y = taps[kw - 1] * window
    for s in range(1, kw):
        y = y + taps[kw - 1 - s] * pltpu.roll(window, s, 0)
    return y[CARRY_ROWS:]


def _const_spec(shape):
    nd = len(shape)
    return pl.BlockSpec(shape, lambda *_: (0,) * nd, pipeline_mode=pl.Buffered(1))


def _attn_in_kernel(h_ref, g_ref, w_ref, wf_ref, bf_ref, qg_ref, kg_ref, hsum_ref, tri_ref, place_ref,
                    qa_ref, qb_ref, k_ref, va_ref, vb_ref, og_ref, kc_ref, a_scr, y_scr, carry_scr,
                    *, tiles_per_batch, pad):
    i = pl.program_id(0)
    tm = h_ref.shape[0]
    row = _row_in_batch(i, tiles_per_batch, tm)
    a = jnp.where(row >= pad, _rmsnorm(h_ref[...], g_ref[...]), 0.0)
    a_scr[...] = a.astype(BF16)

    hd = ATT_HEADS * ATT_HEAD_DIM
    n_chunks = hd // PROJ_CHUNK
    scale = ATT_HEAD_DIM ** -0.5 * LOG2E
    even = (lax.broadcasted_iota(jnp.int32, (1, PROJ_CHUNK), 1) // ATT_HEAD_DIM) % 2 == 0
    gains = (qg_ref, kg_ref, None, None)
    chunks = [(sec, c * PROJ_CHUNK) for sec in range(4) for c in range(n_chunks)]

    def project(idx):
        sec, lo = chunks[idx]
        y_scr[idx % 2] = _mm(a_scr[...], w_ref[:, sec * hd + lo: sec * hd + lo + PROJ_CHUNK])

    def finish(idx):
        sec, lo = chunks[idx]
        cols = slice(lo, lo + PROJ_CHUNK)
        y = y_scr[idx % 2]
        if gains[sec] is not None:
            ms = _sel_right(y * y, hsum_ref[...]) * (1.0 / ATT_HEAD_DIM)
            y = y * lax.rsqrt(ms + EPS) * gains[sec][:, cols]
        if sec == 0:
            y = y * scale
            qa_ref[:, cols] = jnp.where(even, y, 0.0).astype(BF16)
            qb_ref[:, cols] = jnp.where(even, 0.0, y).astype(BF16)
        elif sec == 1:
            k_ref[:, cols] = y.astype(BF16)
        elif sec == 2:
            va_ref[:, cols] = jnp.where(even, y, 1.0).astype(BF16)
            vb_ref[:, cols] = jnp.where(even, 1.0, y).astype(BF16)
        else:
            og_ref[:, cols] = y

    project(0)
    for idx in range(len(chunks)):
        finish(idx)
        if idx + 1 < len(chunks):
            project(idx + 1)

    fl = _mm(a_scr[...], wf_ref[...]) + bf_ref[...]
    logf = -_softplus(-fl) * LOG2E

    @pl.when(i % tiles_per_batch == 0)
    def _():
        carry_scr[...] = jnp.zeros_like(carry_scr)

    c = _sel_left(tri_ref[...], logf) + carry_scr[...]
    carry_scr[...] = c[tm - 1:tm, :]
    kc = sum(_mm(p, place_ref[x]) for x, p in enumerate(_split3(-c)))
    kc_ref[...] = kc.astype(BF16)


def _attn_in(h, g, w4, wf, bfr, qg, kg, hsum, tri, place, *, lp, pad):
    t = h.shape[0]
    tm = ROW_TILE
    tpb = lp // tm
    hd = ATT_HEADS * ATT_HEAD_DIM
    row_spec = pl.BlockSpec((tm, D_MODEL), lambda i: (i, 0))
    out_shapes = (
        jax.ShapeDtypeStruct((t, hd), BF16),
        jax.ShapeDtypeStruct((t, hd), BF16),
        jax.ShapeDtypeStruct((t, hd), BF16),
        jax.ShapeDtypeStruct((t, hd), BF16),
        jax.ShapeDtypeStruct((t, hd), BF16),
        jax.ShapeDtypeStruct((t, hd), F32),
        jax.ShapeDtypeStruct((t, hd), BF16),
    )
    consts = (g, w4, wf, bfr, qg, kg, hsum, tri, place)
    return pl.pallas_call(
        functools.partial(_attn_in_kernel, tiles_per_batch=tpb, pad=pad),
        grid=(t // tm,),
        in_specs=[row_spec] + [_const_spec(c.shape) for c in consts],
        out_specs=(row_spec,) * 7,
        out_shape=out_shapes,
        scratch_shapes=[pltpu.VMEM((tm, D_MODEL), BF16), pltpu.VMEM((2, tm, PROJ_CHUNK), F32),
                        pltpu.VMEM((1, LANES), F32)],
        compiler_params=pltpu.CompilerParams(dimension_semantics=("arbitrary",), vmem_limit_bytes=VMEM_LIMIT),
        name="attn_in",
    )(h, *consts)


def _attn_kernel(qa_ref, qb_ref, k_ref, kc_ref, va_ref, vb_ref, og_ref, o_ref,
                 s_scr, p_scr, alpha_scr, m_scr, acc_scr, *, pad):
    i = pl.program_id(2)
    tq = qa_ref.shape[0]
    tk = ATT_BLOCK
    half = ATT_HEAD_DIM
    lane = lax.broadcasted_iota(jnp.int32, (tq, LANES), 1)
    q_aug = [jnp.concatenate(
        [q_ref[...], jnp.where((lane >= SPLIT * e) & (lane < SPLIT * (e + 1)), 1.0, 0.0).astype(BF16)], axis=1)
        for e, q_ref in enumerate((qa_ref, qb_ref))]
    qpos = i * tq + lax.broadcasted_iota(jnp.int32, (tq, 1), 0)
    m_scr[...] = jnp.full(m_scr.shape, NEG, F32)
    acc_scr[...] = jnp.zeros(acc_scr.shape, F32)

    def block_rows(j):
        return pl.ds(pl.multiple_of(j * tk, tk), tk)

    def logits(j, slot):
        rows = block_rows(j)
        k_aug = jnp.concatenate([k_ref[0, rows, :], kc_ref[0, rows, :]], axis=1)
        for e in range(2):
            s_scr[slot, e] = _mm_nt(q_aug[e], k_aug)

    def softmax(j, slot, masked):
        s_all = s_scr[slot]
        if masked:
            kpos = j * tk + lax.broadcasted_iota(jnp.int32, (1, tk), 1)
            mask = (kpos <= qpos) & (kpos >= pad)
        for e in range(2):
            s = s_all[e]
            if masked:
                s = jnp.where(mask, s, NEG)
            m_old = m_scr[e]
            m_new = jnp.maximum(m_old, jnp.max(s, axis=-1, keepdims=True))
            p_scr[slot, :, e * tk:(e + 1) * tk] = jnp.exp2(s - m_new).astype(BF16)
            alpha_scr[slot, :, e * LANES:(e + 1) * LANES] = jnp.broadcast_to(jnp.exp2(m_old - m_new), (tq, LANES))
            m_scr[e] = m_new

    def values(j, slot):
        rows = block_rows(j)
        nil = jnp.zeros((tk, LANES), BF16)
        vals = jnp.concatenate([jnp.concatenate([va_ref[0, rows, :], nil], axis=1),
                                jnp.concatenate([nil, vb_ref[0, rows, :]], axis=1)], axis=0)
        acc_scr[...] = alpha_scr[slot] * acc_scr[...] + _mm(p_scr[slot], vals)

    def region(j, slot):
        values(j - 1, 1 - slot)
        softmax(j, slot, False)
        logits(j + 1, 1 - slot)

    logits(0, 0)

    @pl.when(i > 0)
    def _():
        logits(1, 1)
        softmax(0, 0, True)

        def pair(t, carry):
            region(2 * t + 1, 1)
            region(2 * t + 2, 0)
            return carry

        lax.fori_loop(0, (i - 1) // 2, pair, 0)

        @pl.when((i - 1) % 2 == 1)
        def _():
            region(i - 1, 1)

    softmax(i, i % 2, True)

    @pl.when(i > 0)
    def _():
        values(i - 1, (i - 1) % 2)

    values(i, i % 2)

    acc = acc_scr[...]
    outs = [acc[:, e * LANES:(e + 1) * LANES] for e in range(2)]
    outs = [x / pltpu.roll(x, half, 1) for x in outs]
    o = jnp.where(lane < half, outs[0], outs[1])
    o_ref[...] = (o * _sigmoid(og_ref[...])).astype(o_ref.dtype)


def _attention(qa, qb, k, kc, va, vb, og, *, batch, lp, pad):
    t, hd = qa.shape
    tq = ATT_BLOCK
    nq = lp // tq
    n_pairs = hd // LANES
    q_spec = pl.BlockSpec((tq, LANES), lambda b, p, i: (b * nq + i, p))
    kv_spec = pl.BlockSpec((1, lp, LANES), lambda b, p, i: (b, 0, p))
    to3 = lambda z: z.reshape(batch, lp, hd)
    return pl.pallas_call(
        functools.partial(_attn_kernel, pad=pad),
        grid=(batch, n_pairs, nq),
        in_specs=[q_spec, q_spec, kv_spec, kv_spec, kv_spec, kv_spec, q_spec],
        out_specs=q_spec,
        out_shape=jax.ShapeDtypeStruct((t, hd), BF16),
        scratch_shapes=[pltpu.VMEM((2, 2, tq, tq), F32), pltpu.VMEM((2, tq, 2 * tq), BF16),
                        pltpu.VMEM((2, tq, 2 * LANES), F32), pltpu.VMEM((2, tq, 1), F32),
                        pltpu.VMEM((tq, 2 * LANES), F32)],
        compiler_params=pltpu.CompilerParams(
            dimension_semantics=("parallel", "parallel", "arbitrary"), vmem_limit_bytes=VMEM_LIMIT),
        name="fox_attention",
    )(qa, qb, to3(k), to3(kc), to3(va), to3(vb), og)


def _post_ffn_kernel(h_ref, mix_ref, wo_ref, gpost_ref, gpre_ref, wup_ref, cw_ref, wdn_ref, gfpost_ref,
                     out_ref, a_scr, u_scr, act_scr, acc_scr, carry_scr, *, tiles_per_batch, pad):
    i = pl.program_id(0)
    tm = h_ref.shape[0]
    row = _row_in_batch(i, tiles_per_batch, tm)
    m = _mm(mix_ref[...], wo_ref[...])
    h1 = h_ref[...] + _rmsnorm(m, gpost_ref[...])
    out_ref[...] = h1
    a = jnp.where(row >= pad, _rmsnorm(h1, gpre_ref[...]), 0.0)
    a_scr[...] = a.astype(BF16)

    @pl.when(i % tiles_per_batch == 0)
    def _():
        carry_scr[...] = jnp.zeros_like(carry_scr)

    fc = FFN_CHUNK
    n_chunks = FFN_DIM // fc

    def project(c):
        for half, base in enumerate((0, FFN_DIM)):
            cols = slice(base + c * fc, base + (c + 1) * fc)
            u_scr[c % 2, half, :CARRY_ROWS, :] = carry_scr[:, cols]
            u_scr[c % 2, half, CARRY_ROWS:, :] = _mm(a_scr[...], wup_ref[:, cols])

    def activate(c):
        taps = [[cw_ref[s:s + 1, base + c * fc:base + (c + 1) * fc] for s in range(FFN_CONV)]
                for base in (0, FFN_DIM)]
        for r in range(0, tm, ELEM_ROWS):
            gate, up = [_causal_conv(u_scr[c % 2, half, r:r + CARRY_ROWS + ELEM_ROWS, :], taps[half])
                        for half in range(2)]
            gelu = 0.5 * gate * (1.0 + jnp.tanh(math.sqrt(2.0 / math.pi) * (gate + 0.044715 * (gate * gate * gate))))
            act_scr[c % 2, r:r + ELEM_ROWS, :] = (gelu * up).astype(BF16)
        for half, base in enumerate((0, FFN_DIM)):
            carry_scr[:, base + c * fc:base + (c + 1) * fc] = u_scr[c % 2, half, tm:, :]

    def contract(c):
        part = _mm(act_scr[c % 2], wdn_ref[c * fc:(c + 1) * fc, :])
        if c == 0:
            acc_scr[...] = part
        else:
            acc_scr[...] += part

    project(0)
    for c in range(n_chunks + 1):
        if c >= 1:
            contract(c - 1)
        if c < n_chunks:
            activate(c)
        if c + 1 < n_chunks:
            project(c + 1)
    out_ref[...] = out_ref[...] + _rmsnorm(acc_scr[...], gfpost_ref[...])


def _post_ffn(h, mix, wo, gpost, gpre, wup, cw, wdn, gfpost, *, lp, pad):
    t = h.shape[0]
    tm = ROW_TILE
    tpb = lp // tm
    row_spec = pl.BlockSpec((tm, D_MODEL), lambda i: (i, 0))
    vec = _const_spec((1, D_MODEL))
    return pl.pallas_call(
        functools.partial(_post_ffn_kernel, tiles_per_batch=tpb, pad=pad),
        grid=(t // tm,),
        in_specs=[row_spec, row_spec, _const_spec(wo.shape), vec, vec, _const_spec(wup.shape),
                  _const_spec(cw.shape), _const_spec(wdn.shape), vec],
        out_specs=row_spec,
        out_shape=jax.ShapeDtypeStruct((t, D_MODEL), F32),
        scratch_shapes=[pltpu.VMEM((tm, D_MODEL), BF16), pltpu.VMEM((2, 2, CARRY_ROWS + tm, FFN_CHUNK), F32),
                        pltpu.VMEM((2, tm, FFN_CHUNK), BF16), pltpu.VMEM((tm, D_MODEL), F32),
                        pltpu.VMEM((CARRY_ROWS, 2 * FFN_DIM), F32)],
        compiler_params=pltpu.CompilerParams(dimension_semantics=("arbitrary",), vmem_limit_bytes=VMEM_LIMIT),
        name="post_ffn",
    )(h, mix, wo, gpost, gpre, wup, cw, wdn, gfpost)


def _dn_in_kernel(h_ref, g_ref, w_ref, cw_ref, wba_ref, wbat_ref, alog_ref, dt_ref, alogt_ref, dtt_ref, tri_ref,
                  q_ref, k_ref, v_ref, og_ref, bg_ref, gct_ref, a_scr, u_scr, carry_scr, *, tiles_per_batch, pad):
    i = pl.program_id(0)
    tm = h_ref.shape[0]
    row = _row_in_batch(i, tiles_per_batch, tm)
    a = jnp.where(row >= pad, _rmsnorm(h_ref[...], g_ref[...]), 0.0)
    a_scr[...] = a.astype(BF16)

    @pl.when(i % tiles_per_batch == 0)
    def _():
        carry_scr[...] = jnp.zeros_like(carry_scr)

    hd = DN_HEADS * DN_HEAD_DIM
    n_chunks = hd // PROJ_CHUNK
    heads_per_chunk = PROJ_CHUNK // DN_HEAD_DIM
    out_refs = (q_ref, k_ref, v_ref, og_ref)
    chunks = [(sec, c * PROJ_CHUNK) for sec in range(4) for c in range(n_chunks)]

    def project(idx):
        sec, lo = chunks[idx]
        cols = slice(sec * hd + lo, sec * hd + lo + PROJ_CHUNK)
        if sec < 3:
            u_scr[idx % 2, :CARRY_ROWS, :] = carry_scr[:, cols]
        u_scr[idx % 2, CARRY_ROWS:, :] = _mm(a_scr[...], w_ref[:, cols])

    def finish(idx):
        sec, lo = chunks[idx]
        if sec == 3:
            og_ref[:, lo:lo + PROJ_CHUNK] = u_scr[idx % 2, CARRY_ROWS:, :]
            return
        cols = slice(sec * hd + lo, sec * hd + lo + PROJ_CHUNK)
        taps = [cw_ref[s:s + 1, cols] for s in range(DN_CONV)]
        for r in range(0, tm, ELEM_ROWS):
            y = _causal_conv(u_scr[idx % 2, r:r + CARRY_ROWS + ELEM_ROWS, :], taps)
            y = y * _sigmoid(y)
            for hh in range(heads_per_chunk):
                yh = y[:, hh * DN_HEAD_DIM:(hh + 1) * DN_HEAD_DIM]
                if sec < 2:
                    yh = yh * lax.rsqrt(jnp.sum(yh * yh, axis=-1, keepdims=True) + EPS)
                    if sec == 0:
                        yh = yh * (DN_HEAD_DIM ** -0.5)
                col = lo + hh * DN_HEAD_DIM
                out_refs[sec][r:r + ELEM_ROWS, col:col + DN_HEAD_DIM] = yh.astype(BF16)
        carry_scr[:, cols] = u_scr[idx % 2, tm:, :]

    project(0)
    for idx in range(len(chunks)):
        finish(idx)
        if idx + 1 < len(chunks):
            project(idx + 1)

    ba = _mm(a_scr[...], wba_ref[...])
    lane = lax.broadcasted_iota(jnp.int32, (1, LANES), 1)
    beta = _sigmoid(ba)
    g = -jnp.exp(alog_ref[...]) * _softplus(ba + dt_ref[...])
    bg_ref[...] = jnp.where(lane < DN_HEADS, beta, _sel_left(tri_ref[...], g))
    bat = _mm_nt(wbat_ref[...], a_scr[...])
    gt = -jnp.exp(alogt_ref[...]) * _softplus(bat + dtt_ref[...])
    gct = _sel_right_t(gt, tri_ref[...])
    for c in range(tm // DN_CHUNK):
        gct_ref[c] = gct[:, c * DN_CHUNK:(c + 1) * DN_CHUNK]


def _dn_in(h, g, w4, cw, wba, wbat, alog, dtb, alogt, dtbt, tri, *, lp, pad):
    t = h.shape[0]
    tm = ROW_TILE
    tpb = lp // tm
    hd = DN_HEADS * DN_HEAD_DIM
    row_spec = pl.BlockSpec((tm, D_MODEL), lambda i: (i, 0))
    out_shapes = (
        jax.ShapeDtypeStruct((t, hd), BF16),
        jax.ShapeDtypeStruct((t, hd), BF16),
        jax.ShapeDtypeStruct((t, hd), BF16),
        jax.ShapeDtypeStruct((t, hd), F32),
        jax.ShapeDtypeStruct((t, LANES), F32),
        jax.ShapeDtypeStruct((t // DN_CHUNK, 2 * DN_HEADS, DN_CHUNK), F32),
    )
    consts = (g, w4, cw, wba, wbat, alog, dtb, alogt, dtbt, tri)
    return pl.pallas_call(
        functools.partial(_dn_in_kernel, tiles_per_batch=tpb, pad=pad),
        grid=(t // tm,),
        in_specs=[row_spec] + [_const_spec(c.shape) for c in consts],
        out_specs=(row_spec, row_spec, row_spec, row_spec, pl.BlockSpec((tm, LANES), lambda i: (i, 0)),
                   pl.BlockSpec((tm // DN_CHUNK, 2 * DN_HEADS, DN_CHUNK), lambda i: (i, 0, 0))),
        out_shape=out_shapes,
        scratch_shapes=[pltpu.VMEM((tm, D_MODEL), BF16), pltpu.VMEM((2, CARRY_ROWS + tm, PROJ_CHUNK), F32),
                        pltpu.VMEM((CARRY_ROWS, 3 * hd), F32)],
        compiler_params=pltpu.CompilerParams(dimension_semantics=("arbitrary",), vmem_limit_bytes=VMEM_LIMIT),
        name="dn_in",
    )(h, *consts)


def _dn_kernel(q_ref, k_ref, v_ref, bg_ref, gct_ref, og_ref, gain_ref, o_ref, s_scr, *, tiles_per_batch):
    i = pl.program_id(0)
    tm = q_ref.shape[0]
    cs = DN_CHUNK
    dh = DN_HEAD_DIM
    heads = range(DN_HEADS)

    @pl.when(i % tiles_per_batch == 0)
    def _():
        s_scr[...] = jnp.zeros_like(s_scr)

    ri = lax.broadcasted_iota(jnp.int32, (cs, cs), 0)
    ci = lax.broadcasted_iota(jnp.int32, (cs, cs), 1)
    incl = ri >= ci
    strict = ri > ci
    blk = (ri // DN_BLOCK) == (ci // DN_BLOCK)

    def chunk(cidx, _):
        r0 = pl.multiple_of(cidx * cs, cs)
        rows = pl.ds(r0, cs)
        bg = bg_ref[rows, :]
        gct = gct_ref[cidx]
        cols = [slice(h * dh, (h + 1) * dh) for h in heads]
        kb = [k_ref[rows, cols[h]] for h in heads]
        qb = [q_ref[rows, cols[h]] for h in heads]
        beta = [bg[:, h:h + 1] for h in heads]
        gc = [bg[:, DN_HEADS + h:DN_HEADS + h + 1] for h in heads]
        dec = [jnp.where(incl, jnp.exp(gc[h] - gct[DN_HEADS + h:DN_HEADS + h + 1, :]), 0.0) for h in heads]
        a_mat = [jnp.where(strict, _mm_nt(kb[h], kb[h]) * dec[h] * beta[h], 0.0) for h in heads]
        off = [jnp.where(blk, 0.0, a_mat[h]) for h in heads]
        mk = [jnp.where(blk, -a_mat[h], 0.0) for h in heads]
        nk = mk
        for _ in range(3):
            mk = [_bmm(mk[h], mk[h]) for h in heads]
            nk = [nk[h] + mk[h] + _bmm(nk[h], mk[h]) for h in heads]
        b = [off[h] + _bmm(nk[h], off[h]) for h in heads]
        qn = [-b[h] for h in heads]
        bp = b
        for _ in range(2):
            bp = [_bmm(bp[h], bp[h]) for h in heads]
            qn = [qn[h] + bp[h] + _bmm(qn[h], bp[h]) for h in heads]
        nt = [qn[h] + nk[h] + _bmm(qn[h], nk[h]) for h in heads]
        egc = [jnp.exp(gc[h]) for h in heads]
        rhs = [jnp.concatenate([kb[h].astype(F32) * (beta[h] * egc[h]),
                                v_ref[rows, cols[h]].astype(F32) * beta[h]], axis=-1) for h in heads]
        wu = [rhs[h] + _bmm(nt[h], rhs[h]) for h in heads]
        qk = [jnp.where(incl, _mm_nt(qb[h], kb[h]) * dec[h], 0.0).astype(BF16) for h in heads]
        s_b = [s_scr[h].astype(BF16) for h in heads]
        u = [wu[h][:, dh:] - _bmm(wu[h][:, :dh], s_b[h]) for h in heads]
        ub = [u[h].astype(BF16) for h in heads]
        o = [_bmm(qb[h].astype(F32) * egc[h], s_b[h]) + _mm(qk[h], ub[h]) for h in heads]
        g_last = [gc[h][cs - 1:cs, :] for h in heads]
        kd = [(kb[h].astype(F32) * jnp.exp(g_last[h] - gc[h])).astype(BF16) for h in heads]
        for h in heads:
            s_scr[h] = jnp.exp(g_last[h]) * s_scr[h] + _mm_tn(kd[h], ub[h])
            og = og_ref[rows, cols[h]]
            on = _rmsnorm(o[h], gain_ref[...]) * (og * _sigmoid(og))
            o_ref[rows, cols[h]] = on.astype(o_ref.dtype)
        return 0

    lax.fori_loop(0, tm // cs, chunk, 0)


def _deltanet(q, k, v, bg, gct, og, gain, *, lp):
    t, hd = q.shape
    tm = ROW_TILE
    tpb = lp // tm
    row_spec = pl.BlockSpec((tm, hd), lambda i: (i, 0))
    return pl.pallas_call(
        functools.partial(_dn_kernel, tiles_per_batch=tpb),
        grid=(t // tm,),
        in_specs=[row_spec, row_spec, row_spec, pl.BlockSpec((tm, LANES), lambda i: (i, 0)),
                  pl.BlockSpec((tm // DN_CHUNK, 2 * DN_HEADS, DN_CHUNK), lambda i: (i, 0, 0)), row_spec,
                  _const_spec((1, DN_HEAD_DIM))],
        out_specs=row_spec,
        out_shape=jax.ShapeDtypeStruct((t, hd), BF16),
        scratch_shapes=[pltpu.VMEM((DN_HEADS, DN_HEAD_DIM, DN_HEAD_DIM), F32)],
        compiler_params=pltpu.CompilerParams(dimension_semantics=("arbitrary",), vmem_limit_bytes=VMEM_LIMIT),
        name="gated_deltanet",
    )(q, k, v, bg, gct, og, gain)


def _pad_cols(w, n):
    return jnp.pad(w, ((0, 0), (0, n - w.shape[1])))


def kernel(x, meta_tokens, norm_mix_pre, norm_mix_post, norm_ffn_pre, norm_ffn_post, attn_w_in, attn_b_forget, attn_q_norm, attn_k_norm, attn_w_out, dn_w_in, dn_conv, dn_a_log, dn_dt_bias, dn_o_norm, dn_w_out, ffn_w_up, ffn_conv, ffn_w_down):
    batch, seq, d = x.shape
    assert d == D_MODEL
    depth = norm_mix_pre.shape[0]
    length = N_META + seq
    lp = -(-length // ROW_TILE) * ROW_TILE
    pad = lp - length

    meta = jnp.broadcast_to(meta_tokens[None].astype(x.dtype), (batch, N_META, d))
    h = jnp.concatenate([jnp.zeros((batch, pad, d), x.dtype), meta, x], axis=1).reshape(batch * lp, d)

    att_hd = ATT_HEADS * ATT_HEAD_DIM
    dn_hd = DN_HEADS * DN_HEAD_DIM
    vec = lambda g: g.reshape(1, -1).astype(F32)
    iota = lambda n, ax: lax.broadcasted_iota(jnp.int32, (n, n), ax)
    head_sum = (iota(PROJ_CHUNK, 0) // ATT_HEAD_DIM == iota(PROJ_CHUNK, 1) // ATT_HEAD_DIM).astype(BF16)
    lower = iota(ROW_TILE, 0) >= iota(ROW_TILE, 1)
    tri = lower.astype(BF16)
    tri_chunk = (lower & (iota(ROW_TILE, 0) // DN_CHUNK == iota(ROW_TILE, 1) // DN_CHUNK)).astype(BF16)
    hh = lax.broadcasted_iota(jnp.int32, (SPLIT, LANES, att_hd), 1)
    tgt = lax.broadcasted_iota(jnp.int32, (SPLIT, LANES, att_hd), 2)
    piece = lax.broadcasted_iota(jnp.int32, (SPLIT, LANES, att_hd), 0)
    place = ((hh < ATT_HEADS) & (tgt == LANES * (hh // 2) + SPLIT * (hh % 2) + piece)).astype(BF16)

    for i in range(depth):
        j = i // 2
        if i % 2 == 0:
            w_in = attn_w_in[j]
            qa, qb, k, va, vb, og, kc = _attn_in(
                h, vec(norm_mix_pre[i]), w_in[:, :4 * att_hd].astype(BF16),
                _pad_cols(w_in[:, 4 * att_hd:], LANES).astype(BF16), _pad_cols(vec(attn_b_forget[j]), LANES),
                vec(jnp.tile(attn_q_norm[j], ATT_HEADS)), vec(jnp.tile(attn_k_norm[j], ATT_HEADS)),
                head_sum, tri, place, lp=lp, pad=pad)
            mix = _attention(qa, qb, k, kc, va, vb, og, batch=batch, lp=lp, pad=pad)
            w_out = attn_w_out[j]
        else:
            w_in = dn_w_in[j]
            zeros = jnp.zeros((DN_HEADS,), F32)
            wba = w_in[:, 4 * dn_hd:]
            alog = jnp.concatenate([zeros, dn_a_log[j]]).astype(F32)
            dtb = jnp.concatenate([zeros, dn_dt_bias[j]]).astype(F32)
            q, k, v, og, bg, gct = _dn_in(
                h, vec(norm_mix_pre[i]), w_in[:, :4 * dn_hd].astype(BF16), dn_conv[j].astype(F32),
                _pad_cols(wba, LANES).astype(BF16), wba.T.astype(BF16),
                _pad_cols(vec(alog), LANES), _pad_cols(vec(dtb), LANES), alog.reshape(-1, 1), dtb.reshape(-1, 1),
                tri_chunk, lp=lp, pad=pad)
            mix = _deltanet(q, k, v, bg, gct, og, vec(dn_o_norm[j]), lp=lp)
            w_out = dn_w_out[j]
        h = _post_ffn(h, mix, w_out.astype(BF16), vec(norm_mix_post[i]), vec(norm_ffn_pre[i]),
                      ffn_w_up[i].astype(BF16), ffn_conv[i].astype(F32), ffn_w_down[i].astype(BF16),
                      vec(norm_ffn_post[i]), lp=lp, pad=pad)
    return h.reshape(batch, lp, d)[:, pad + N_META:]
```

```python
import functools
import math

import jax
import jax.numpy as jnp
from jax import lax
from jax.experimental import pallas as pl
from jax.experimental.pallas import tpu as pltpu

D_MODEL = 1024
N_META = 16
ATT_HEADS = 16
ATT_HEAD_DIM = 64
DN_HEADS = 8
DN_HEAD_DIM = 128
DN_CONV = 4
DN_CHUNK = 128
DN_BLOCK = 16
FFN_DIM = 2816
FFN_CONV = 3
EPS = 1e-6
NEG = -1e30
LOG2E = 1.4426950408889634

LANES = 128
ROW_TILE = 768
ATT_BLOCK = ROW_TILE
FFN_CHUNK = 256
PROJ_CHUNK = 512
CARRY_ROWS = 8
ELEM_ROWS = 64
SPLIT = 3
VMEM_LIMIT = 56 * 1024 * 1024

F32 = jnp.float32
BF16 = jnp.bfloat16


def _rmsnorm(x, g):
    return x * lax.rsqrt(jnp.mean(x * x, axis=-1, keepdims=True) + EPS) * g


def _mm(a, b):
    return jnp.dot(a, b, preferred_element_type=F32)


def _mm_nt(a, b):
    return lax.dot_general(a, b, (((1,), (1,)), ((), ())), preferred_element_type=F32)


def _mm_tn(a, b):
    return lax.dot_general(a, b, (((0,), (0,)), ((), ())), preferred_element_type=F32)


def _bmm(a, b):
    return _mm(a.astype(BF16), b.astype(BF16))


def _split3(x):
    hi = x.astype(BF16)
    r = x - hi.astype(F32)
    mid = r.astype(BF16)
    lo = (r - mid.astype(F32)).astype(BF16)
    return hi, mid, lo


def _sel_right(x, sel):
    return sum(_mm(p, sel) for p in _split3(x))


def _sel_right_t(x, sel):
    return sum(_mm_nt(p, sel) for p in _split3(x))


def _sel_left(sel, x):
    return sum(_mm(sel, p) for p in _split3(x))


def _softplus(x):
    return jnp.maximum(x, 0.0) + jnp.log1p(jnp.exp(-jnp.abs(x)))


def _sigmoid(x):
    return 1.0 / (1.0 + jnp.exp(-x))


def _row_in_batch(i, tiles_per_batch, rows):
    base = (i % tiles_per_batch) * rows
    return base + lax.broadcasted_iota(jnp.int32, (rows, 1), 0)


def _causal_conv(window, taps):
    kw = len(taps)
    y = taps[kw - 1] * window
    for s in range(1, kw):
        y = y + taps[kw - 1 - s] * pltpu.roll(window, s, 0)
    return y[CARRY_ROWS:]


def _const_spec(shape):
    nd = len(shape)
    return pl.BlockSpec(shape, lambda *_: (0,) * nd, pipeline_mode=pl.Buffered(1))


def _attn_in_kernel(h_ref, g_ref, w_ref, wf_ref, bf_ref, qg_ref, kg_ref, hsum_ref, tri_ref, place_ref,
                    qa_ref, qb_ref, k_ref, va_ref, vb_ref, og_ref, kc_ref, a_scr, y_scr, carry_scr,
                    *, tiles_per_batch, pad):
    i = pl.program_id(0)
    tm = h_ref.shape[0]
    row = _row_in_batch(i, tiles_per_batch, tm)
    a = jnp.where(row >= pad, _rmsnorm(h_ref[...], g_ref[...]), 0.0)
    a_scr[...] = a.astype(BF16)

    hd = ATT_HEADS * ATT_HEAD_DIM
    n_chunks = hd // PROJ_CHUNK
    scale = ATT_HEAD_DIM ** -0.5 * LOG2E
    even = (lax.broadcasted_iota(jnp.int32, (1, PROJ_CHUNK), 1) // ATT_HEAD_DIM) % 2 == 0
    gains = (qg_ref, kg_ref, None, None)
    chunks = [(sec, c * PROJ_CHUNK) for sec in range(4) for c in range(n_chunks)]

    def project(idx):
        sec, lo = chunks[idx]
        y_scr[idx % 2] = _mm(a_scr[...], w_ref[:, sec * hd + lo: sec * hd + lo + PROJ_CHUNK])

    def finish(idx):
        sec, lo = chunks[idx]
        cols = slice(lo, lo + PROJ_CHUNK)
        y = y_scr[idx % 2]
        if gains[sec] is not None:
            ms = _sel_right(y * y, hsum_ref[...]) * (1.0 / ATT_HEAD_DIM)
            y = y * lax.rsqrt(ms + EPS) * gains[sec][:, cols]
        if sec == 0:
            y = y * scale
            qa_ref[:, cols] = jnp.where(even, y, 0.0).astype(BF16)
            qb_ref[:, cols] = jnp.where(even, 0.0, y).astype(BF16)
        elif sec == 1:
            k_ref[:, cols] = y.astype(BF16)
        elif sec == 2:
            va_ref[:, cols] = jnp.where(even, y, 1.0).astype(BF16)
            vb_ref[:, cols] = jnp.where(even, 1.0, y).astype(BF16)
        else:
            og_ref[:, cols] = y

    project(0)
    for idx in range(len(chunks)):
        finish(idx)
        if idx + 1 < len(chunks):
            project(idx + 1)

    fl = _mm(a_scr[...], wf_ref[...]) + bf_ref[...]
    logf = -_softplus(-fl) * LOG2E

    @pl.when(i % tiles_per_batch == 0)
    def _():
        carry_scr[...] = jnp.zeros_like(carry_scr)

    c = _sel_left(tri_ref[...], logf) + carry_scr[...]
    carry_scr[...] = c[tm - 1:tm, :]
    kc = sum(_mm(p, place_ref[x]) for x, p in enumerate(_split3(-c)))
    kc_ref[...] = kc.astype(BF16)


def _attn_in(h, g, w4, wf, bfr, qg, kg, hsum, tri, place, *, lp, pad):
    t = h.shape[0]
    tm = ROW_TILE
    tpb = lp // tm
    hd = ATT_HEADS * ATT_HEAD_DIM
    row_spec = pl.BlockSpec((tm, D_MODEL), lambda i: (i, 0))
    out_shapes = (
        jax.ShapeDtypeStruct((t, hd), BF16),
        jax.ShapeDtypeStruct((t, hd), BF16),
        jax.ShapeDtypeStruct((t, hd), BF16),
        jax.ShapeDtypeStruct((t, hd), BF16),
        jax.ShapeDtypeStruct((t, hd), BF16),
        jax.ShapeDtypeStruct((t, hd), F32),
        jax.ShapeDtypeStruct((t, hd), BF16),
    )
    consts = (g, w4, wf, bfr, qg, kg, hsum, tri, place)
    return pl.pallas_call(
        functools.partial(_attn_in_kernel, tiles_per_batch=tpb, pad=pad),
        grid=(t // tm,),
        in_specs=[row_spec] + [_const_spec(c.shape) for c in consts],
        out_specs=(row_spec,) * 7,
        out_shape=out_shapes,
        scratch_shapes=[pltpu.VMEM((tm, D_MODEL), BF16), pltpu.VMEM((2, tm, PROJ_CHUNK), F32),
                        pltpu.VMEM((1, LANES), F32)],
        compiler_params=pltpu.CompilerParams(dimension_semantics=("arbitrary",), vmem_limit_bytes=VMEM_LIMIT),
        name="attn_in",
    )(h, *consts)


def _attn_kernel(qa_ref, qb_ref, k_ref, kc_ref, va_ref, vb_ref, og_ref, o_ref,
                 s_scr, p_scr, alpha_scr, m_scr, acc_scr, *, pad, n_blocks):
    tq = ATT_BLOCK
    half = ATT_HEAD_DIM
    n_steps = n_blocks * (n_blocks + 1) // 2
    lane = lax.broadcasted_iota(jnp.int32, (tq, LANES), 1)
    ones_sel = [jnp.where((lane >= SPLIT * e) & (lane < SPLIT * (e + 1)), 1.0, 0.0).astype(BF16) for e in range(2)]
    m_scr[...] = jnp.full(m_scr.shape, NEG, F32)
    acc_scr[...] = jnp.zeros(acc_scr.shape, F32)

    def rows(b):
        return pl.ds(pl.multiple_of(b * tq, tq), tq)

    def advance(ij):
        i, j = ij
        return jnp.where(j == i, i + 1, i), jnp.where(j == i, 0, j + 1)

    def logits(ij, slot):
        i, j = ij
        k_aug = jnp.concatenate([k_ref[0, rows(j), :], kc_ref[0, rows(j), :]], axis=1)
        for e, q_ref in enumerate((qa_ref, qb_ref)):
            s_scr[slot, e] = _mm_nt(jnp.concatenate([q_ref[0, rows(i), :], ones_sel[e]], axis=1), k_aug)

    def softmax(ij, slot):
        i, j = ij
        kpos = j * tq + lax.broadcasted_iota(jnp.int32, (1, tq), 1)
        kpos = jnp.where(kpos < pad, jnp.int32(2 ** 30), kpos)
        mask = kpos <= i * tq + lax.broadcasted_iota(jnp.int32, (tq, 1), 0)
        for e in range(2):
            s = jnp.where(mask, s_scr[slot, e], NEG)
            m_old = jnp.where(j == 0, NEG, m_scr[e])
            m_new = jnp.maximum(m_old, jnp.max(s, axis=-1, keepdims=True))
            p_scr[slot, :, e * tq:(e + 1) * tq] = jnp.exp2(s - m_new).astype(BF16)
            alpha_scr[slot, :, e * LANES:(e + 1) * LANES] = jnp.broadcast_to(jnp.exp2(m_old - m_new), (tq, LANES))
            m_scr[e] = m_new

    def values(ij, slot):
        _, j = ij
        nil = jnp.zeros((tq, LANES), BF16)
        vals = jnp.concatenate([jnp.concatenate([va_ref[0, rows(j), :], nil], axis=1),
                                jnp.concatenate([nil, vb_ref[0, rows(j), :]], axis=1)], axis=0)
        acc_scr[...] = alpha_scr[slot] * acc_scr[...] + _mm(p_scr[slot], vals)

    def finalize(i):
        acc = acc_scr[...]
        outs = [acc[:, e * LANES:(e + 1) * LANES] for e in range(2)]
        outs = [x / pltpu.roll(x, half, 1) for x in outs]
        o = jnp.where(lane < half, outs[0], outs[1])
        o_ref[0, rows(i), :] = (o * _sigmoid(og_ref[0, rows(i), :])).astype(o_ref.dtype)

    def finalize_if_last(ij):
        i, j = ij

        @pl.when(j == i)
        def _():
            finalize(i)

    def region(prev, cur, nxt, slot):
        values(prev, 1 - slot)
        softmax(cur, slot)
        logits(nxt, 1 - slot)
        finalize_if_last(prev)

    first = (jnp.int32(0), jnp.int32(0))
    logits(first, 0)
    if n_steps == 1:
        softmax(first, 0)
        values(first, 0)
    else:
        second = advance(first)
        softmax(first, 0)
        logits(second, 1)

        def two_steps(_, carry):
            prev, cur = carry
            nxt = advance(cur)
            region(prev, cur, nxt, 1)
            nxt2 = advance(nxt)
            region(cur, nxt, nxt2, 0)
            return nxt, nxt2

        n_mid = n_steps - 2
        prev, cur = lax.fori_loop(0, n_mid // 2, two_steps, (first, second))
        if n_mid % 2:
            nxt = advance(cur)
            region(prev, cur, nxt, 1)
            prev, cur = cur, nxt
        slot = (n_steps - 1) % 2
        values(prev, 1 - slot)
        softmax(cur, slot)
        finalize_if_last(prev)
        values(cur, slot)
    finalize(n_blocks - 1)


def _attention(qa, qb, k, kc, va, vb, og, *, batch, lp, pad):
    t, hd = qa.shape
    tq = ATT_BLOCK
    n_pairs = hd // LANES
    spec = pl.BlockSpec((1, lp, LANES), lambda b, p: (b, 0, p))
    once = pl.BlockSpec((1, lp, LANES), lambda b, p: (b, 0, p), pipeline_mode=pl.Buffered(1))
    to3 = lambda z: z.reshape(batch, lp, hd)
    out = pl.pallas_call(
        functools.partial(_attn_kernel, pad=pad, n_blocks=lp // tq),
        grid=(batch, n_pairs),
        in_specs=[spec, spec, once, once, once, once, once],
        out_specs=spec,
        out_shape=jax.ShapeDtypeStruct((batch, lp, hd), BF16),
        scratch_shapes=[pltpu.VMEM((2, 2, tq, tq), F32), pltpu.VMEM((2, tq, 2 * tq), BF16),
                        pltpu.VMEM((2, tq, 2 * LANES), F32), pltpu.VMEM((2, tq, 1), F32),
                        pltpu.VMEM((tq, 2 * LANES), F32)],
        compiler_params=pltpu.CompilerParams(
            dimension_semantics=("parallel", "parallel"), vmem_limit_bytes=VMEM_LIMIT),
        name="fox_attention",
    )(to3(qa), to3(qb), to3(k), to3(kc), to3(va), to3(vb), to3(og))
    return out.reshape(t, hd)


def _post_ffn_kernel(h_ref, mix_ref, wo_ref, gpost_ref, gpre_ref, wup_ref, cw_ref, wdn_ref, gfpost_ref,
                     out_ref, a_scr, u_scr, act_scr, acc_scr, carry_scr, *, tiles_per_batch, pad):
    i = pl.program_id(0)
    tm = h_ref.shape[0]
    row = _row_in_batch(i, tiles_per_batch, tm)
    m = _mm(mix_ref[...], wo_ref[...])
    h1 = h_ref[...] + _rmsnorm(m, gpost_ref[...])
    out_ref[...] = h1
    a = jnp.where(row >= pad, _rmsnorm(h1, gpre_ref[...]), 0.0)
    a_scr[...] = a.astype(BF16)

    @pl.when(i % tiles_per_batch == 0)
    def _():
        carry_scr[...] = jnp.zeros_like(carry_scr)

    fc = FFN_CHUNK
    n_chunks = FFN_DIM // fc

    def project(c):
        for half, base in enumerate((0, FFN_DIM)):
            cols = slice(base + c * fc, base + (c + 1) * fc)
            u_scr[c % 2, half, :CARRY_ROWS, :] = carry_scr[:, cols]
            u_scr[c % 2, half, CARRY_ROWS:, :] = _mm(a_scr[...], wup_ref[:, cols])

    def activate(c):
        taps = [[cw_ref[s:s + 1, base + c * fc:base + (c + 1) * fc] for s in range(FFN_CONV)]
                for base in (0, FFN_DIM)]
        for r in range(0, tm, ELEM_ROWS):
            gate, up = [_causal_conv(u_scr[c % 2, half, r:r + CARRY_ROWS + ELEM_ROWS, :], taps[half])
                        for half in range(2)]
            gelu = 0.5 * gate * (1.0 + jnp.tanh(math.sqrt(2.0 / math.pi) * (gate + 0.044715 * (gate * gate * gate))))
            act_scr[c % 2, r:r + ELEM_ROWS, :] = (gelu * up).astype(BF16)
        for half, base in enumerate((0, FFN_DIM)):
            carry_scr[:, base + c * fc:base + (c + 1) * fc] = u_scr[c % 2, half, tm:, :]

    def contract(c):
        part = _mm(act_scr[c % 2], wdn_ref[c * fc:(c + 1) * fc, :])
        if c == 0:
            acc_scr[...] = part
        else:
            acc_scr[...] += part

    project(0)
    for c in range(n_chunks + 1):
        if c >= 1:
            contract(c - 1)
        if c < n_chunks:
            activate(c)
        if c + 1 < n_chunks:
            project(c + 1)
    out_ref[...] = out_ref[...] + _rmsnorm(acc_scr[...], gfpost_ref[...])


def _post_ffn(h, mix, wo, gpost, gpre, wup, cw, wdn, gfpost, *, lp, pad):
    t = h.shape[0]
    tm = ROW_TILE
    tpb = lp // tm
    row_spec = pl.BlockSpec((tm, D_MODEL), lambda i: (i, 0))
    vec = _const_spec((1, D_MODEL))
    return pl.pallas_call(
        functools.partial(_post_ffn_kernel, tiles_per_batch=tpb, pad=pad),
        grid=(t // tm,),
        in_specs=[row_spec, row_spec, _const_spec(wo.shape), vec, vec, _const_spec(wup.shape),
                  _const_spec(cw.shape), _const_spec(wdn.shape), vec],
        out_specs=row_spec,
        out_shape=jax.ShapeDtypeStruct((t, D_MODEL), F32),
        scratch_shapes=[pltpu.VMEM((tm, D_MODEL), BF16), pltpu.VMEM((2, 2, CARRY_ROWS + tm, FFN_CHUNK), F32),
                        pltpu.VMEM((2, tm, FFN_CHUNK), BF16), pltpu.VMEM((tm, D_MODEL), F32),
                        pltpu.VMEM((CARRY_ROWS, 2 * FFN_DIM), F32)],
        compiler_params=pltpu.CompilerParams(dimension_semantics=("arbitrary",), vmem_limit_bytes=VMEM_LIMIT),
        name="post_ffn",
    )(h, mix, wo, gpost, gpre, wup, cw, wdn, gfpost)


def _dn_in_kernel(h_ref, g_ref, w_ref, cw_ref, wba_ref, wbat_ref, alog_ref, dt_ref, alogt_ref, dtt_ref, tri_ref,
                  q_ref, k_ref, v_ref, og_ref, bg_ref, gct_ref, a_scr, u_scr, carry_scr, *, tiles_per_batch, pad):
    i = pl.program_id(0)
    tm = h_ref.shape[0]
    row = _row_in_batch(i, tiles_per_batch, tm)
    a = jnp.where(row >= pad, _rmsnorm(h_ref[...], g_ref[...]), 0.0)
    a_scr[...] = a.astype(BF16)

    @pl.when(i % tiles_per_batch == 0)
    def _():
        carry_scr[...] = jnp.zeros_like(carry_scr)

    hd = DN_HEADS * DN_HEAD_DIM
    n_chunks = hd // PROJ_CHUNK
    heads_per_chunk = PROJ_CHUNK // DN_HEAD_DIM
    out_refs = (q_ref, k_ref, v_ref, og_ref)
    chunks = [(sec, c * PROJ_CHUNK) for sec in range(4) for c in range(n_chunks)]

    def project(idx):
        sec, lo = chunks[idx]
        cols = slice(sec * hd + lo, sec * hd + lo + PROJ_CHUNK)
        if sec < 3:
            u_scr[idx % 2, :CARRY_ROWS, :] = carry_scr[:, cols]
        u_scr[idx % 2, CARRY_ROWS:, :] = _mm(a_scr[...], w_ref[:, cols])

    def finish(idx):
        sec, lo = chunks[idx]
        if sec == 3:
            og_ref[:, lo:lo + PROJ_CHUNK] = u_scr[idx % 2, CARRY_ROWS:, :]
            return
        cols = slice(sec * hd + lo, sec * hd + lo + PROJ_CHUNK)
        taps = [cw_ref[s:s + 1, cols] for s in range(DN_CONV)]
        for r in range(0, tm, ELEM_ROWS):
            y = _causal_conv(u_scr[idx % 2, r:r + CARRY_ROWS + ELEM_ROWS, :], taps)
            y = y * _sigmoid(y)
            for hh in range(heads_per_chunk):
                yh = y[:, hh * DN_HEAD_DIM:(hh + 1) * DN_HEAD_DIM]
                if sec < 2:
                    yh = yh * lax.rsqrt(jnp.sum(yh * yh, axis=-1, keepdims=True) + EPS)
                    if sec == 0:
                        yh = yh * (DN_HEAD_DIM ** -0.5)
                col = lo + hh * DN_HEAD_DIM
                out_refs[sec][r:r + ELEM_ROWS, col:col + DN_HEAD_DIM] = yh.astype(BF16)
        carry_scr[:, cols] = u_scr[idx % 2, tm:, :]

    project(0)
    for idx in range(len(chunks)):
        finish(idx)
        if idx + 1 < len(chunks):
            project(idx + 1)

    ba = _mm(a_scr[...], wba_ref[...])
    lane = lax.broadcasted_iota(jnp.int32, (1, LANES), 1)
    beta = _sigmoid(ba)
    g = -jnp.exp(alog_ref[...]) * _softplus(ba + dt_ref[...])
    bg_ref[...] = jnp.where(lane < DN_HEADS, beta, _sel_left(tri_ref[...], g))
    bat = _mm_nt(wbat_ref[...], a_scr[...])
    gt = -jnp.exp(alogt_ref[...]) * _softplus(bat + dtt_ref[...])
    gct = _sel_right_t(gt, tri_ref[...])
    for c in range(tm // DN_CHUNK):
        gct_ref[c] = gct[:, c * DN_CHUNK:(c + 1) * DN_CHUNK]


def _dn_in(h, g, w4, cw, wba, wbat, alog, dtb, alogt, dtbt, tri, *, lp, pad):
    t = h.shape[0]
    tm = ROW_TILE
    tpb = lp // tm
    hd = DN_HEADS * DN_HEAD_DIM
    row_spec = pl.BlockSpec((tm, D_MODEL), lambda i: (i, 0))
    out_shapes = (
        jax.ShapeDtypeStruct((t, hd), BF16),
        jax.ShapeDtypeStruct((t, hd), BF16),
        jax.ShapeDtypeStruct((t, hd), BF16),
        jax.ShapeDtypeStruct((t, hd), F32),
        jax.ShapeDtypeStruct((t, LANES), F32),
        jax.ShapeDtypeStruct((t // DN_CHUNK, 2 * DN_HEADS, DN_CHUNK), F32),
    )
    consts = (g, w4, cw, wba, wbat, alog, dtb, alogt, dtbt, tri)
    return pl.pallas_call(
        functools.partial(_dn_in_kernel, tiles_per_batch=tpb, pad=pad),
        grid=(t // tm,),
        in_specs=[row_spec] + [_const_spec(c.shape) for c in consts],
        out_specs=(row_spec, row_spec, row_spec, row_spec, pl.BlockSpec((tm, LANES), lambda i: (i, 0)),
                   pl.BlockSpec((tm // DN_CHUNK, 2 * DN_HEADS, DN_CHUNK), lambda i: (i, 0, 0))),
        out_shape=out_shapes,
        scratch_shapes=[pltpu.VMEM((tm, D_MODEL), BF16), pltpu.VMEM((2, CARRY_ROWS + tm, PROJ_CHUNK), F32),
                        pltpu.VMEM((CARRY_ROWS, 3 * hd), F32)],
        compiler_params=pltpu.CompilerParams(dimension_semantics=("arbitrary",), vmem_limit_bytes=VMEM_LIMIT),
        name="dn_in",
    )(h, *consts)


def _dn_kernel(q_ref, k_ref, v_ref, bg_ref, gct_ref, og_ref, gain_ref, o_ref, s_scr, *, tiles_per_batch):
    i = pl.program_id(0)
    tm = q_ref.shape[0]
    cs = DN_CHUNK
    dh = DN_HEAD_DIM
    heads = range(DN_HEADS)

    @pl.when(i % tiles_per_batch == 0)
    def _():
        s_scr[...] = jnp.zeros_like(s_scr)

    ri = lax.broadcasted_iota(jnp.int32, (cs, cs), 0)
    ci = lax.broadcasted_iota(jnp.int32, (cs, cs), 1)
    incl = ri >= ci
    strict = ri > ci
    blk = (ri // DN_BLOCK) == (ci // DN_BLOCK)

    def chunk(cidx, _):
        r0 = pl.multiple_of(cidx * cs, cs)
        rows = pl.ds(r0, cs)
        bg = bg_ref[rows, :]
        gct = gct_ref[cidx]
        cols = [slice(h * dh, (h + 1) * dh) for h in heads]
        kb = [k_ref[rows, cols[h]] for h in heads]
        qb = [q_ref[rows, cols[h]] for h in heads]
        beta = [bg[:, h:h + 1] for h in heads]
        gc = [bg[:, DN_HEADS + h:DN_HEADS + h + 1] for h in heads]
        dec = [jnp.where(incl, jnp.exp(gc[h] - gct[DN_HEADS + h:DN_HEADS + h + 1, :]), 0.0) for h in heads]
        a_mat = [jnp.where(strict, _mm_nt(kb[h], kb[h]) * dec[h] * beta[h], 0.0) for h in heads]
        off = [jnp.where(blk, 0.0, a_mat[h]) for h in heads]
        mk = [jnp.where(blk, -a_mat[h], 0.0) for h in heads]
        nk = mk
        for _ in range(3):
            mk = [_bmm(mk[h], mk[h]) for h in heads]
            nk = [nk[h] + mk[h] + _bmm(nk[h], mk[h]) for h in heads]
        b = [off[h] + _bmm(nk[h], off[h]) for h in heads]
        qn = [-b[h] for h in heads]
        bp = b
        for _ in range(2):
            bp = [_bmm(bp[h], bp[h]) for h in heads]
            qn = [qn[h] + bp[h] + _bmm(qn[h], bp[h]) for h in heads]
        nt = [qn[h] + nk[h] + _bmm(qn[h], nk[h]) for h in heads]
        egc = [jnp.exp(gc[h]) for h in heads]
        rhs = [jnp.concatenate([kb[h].astype(F32) * (beta[h] * egc[h]),
                                v_ref[rows, cols[h]].astype(F32) * beta[h]], axis=-1) for h in heads]
        wu = [rhs[h] + _bmm(nt[h], rhs[h]) for h in heads]
        qk = [jnp.where(incl, _mm_nt(qb[h], kb[h]) * dec[h], 0.0).astype(BF16) for h in heads]
        s_b = [s_scr[h].astype(BF16) for h in heads]
        u = [wu[h][:, dh:] - _bmm(wu[h][:, :dh], s_b[h]) for h in heads]
        ub = [u[h].astype(BF16) for h in heads]
        o = [_bmm(qb[h].astype(F32) * egc[h], s_b[h]) + _mm(qk[h], ub[h]) for h in heads]
        g_last = [gc[h][cs - 1:cs, :] for h in heads]
        kd = [(kb[h].astype(F32) * jnp.exp(g_last[h] - gc[h])).astype(BF16) for h in heads]
        for h in heads:
            s_scr[h] = jnp.exp(g_last[h]) * s_scr[h] + _mm_tn(kd[h], ub[h])
            og = og_ref[rows, cols[h]]
            on = _rmsnorm(o[h], gain_ref[...]) * (og * _sigmoid(og))
            o_ref[rows, cols[h]] = on.astype(o_ref.dtype)
        return 0

    lax.fori_loop(0, tm // cs, chunk, 0)


def _deltanet(q, k, v, bg, gct, og, gain, *, lp):
    t, hd = q.shape
    tm = ROW_TILE
    tpb = lp // tm
    row_spec = pl.BlockSpec((tm, hd), lambda i: (i, 0))
    return pl.pallas_call(
        functools.partial(_dn_kernel, tiles_per_batch=tpb),
        grid=(t // tm,),
        in_specs=[row_spec, row_spec, row_spec, pl.BlockSpec((tm, LANES), lambda i: (i, 0)),
                  pl.BlockSpec((tm // DN_CHUNK, 2 * DN_HEADS, DN_CHUNK), lambda i: (i, 0, 0)), row_spec,
                  _const_spec((1, DN_HEAD_DIM))],
        out_specs=row_spec,
        out_shape=jax.ShapeDtypeStruct((t, hd), BF16),
        scratch_shapes=[pltpu.VMEM((DN_HEADS, DN_HEAD_DIM, DN_HEAD_DIM), F32)],
        compiler_params=pltpu.CompilerParams(dimension_semantics=("arbitrary",), vmem_limit_bytes=VMEM_LIMIT),
        name="gated_deltanet",
    )(q, k, v, bg, gct, og, gain)


def _pad_cols(w, n):
    return jnp.pad(w, ((0, 0), (0, n - w.shape[1])))


def kernel(x, meta_tokens, norm_mix_pre, norm_mix_post, norm_ffn_pre, norm_ffn_post, attn_w_in, attn_b_forget, attn_q_norm, attn_k_norm, attn_w_out, dn_w_in, dn_conv, dn_a_log, dn_dt_bias, dn_o_norm, dn_w_out, ffn_w_up, ffn_conv, ffn_w_down):
    batch, seq, d = x.shape
    assert d == D_MODEL
    depth = norm_mix_pre.shape[0]
    length = N_META + seq
    lp = -(-length // ROW_TILE) * ROW_TILE
    pad = lp - length

    meta = jnp.broadcast_to(meta_tokens[None].astype(x.dtype), (batch, N_META, d))
    h = jnp.concatenate([jnp.zeros((batch, pad, d), x.dtype), meta, x], axis=1).reshape(batch * lp, d)

    att_hd = ATT_HEADS * ATT_HEAD_DIM
    dn_hd = DN_HEADS * DN_HEAD_DIM
    vec = lambda g: g.reshape(1, -1).astype(F32)
    iota = lambda n, ax: lax.broadcasted_iota(jnp.int32, (n, n), ax)
    head_sum = (iota(PROJ_CHUNK, 0) // ATT_HEAD_DIM == iota(PROJ_CHUNK, 1) // ATT_HEAD_DIM).astype(BF16)
    lower = iota(ROW_TILE, 0) >= iota(ROW_TILE, 1)
    tri = lower.astype(BF16)
    tri_chunk = (lower & (iota(ROW_TILE, 0) // DN_CHUNK == iota(ROW_TILE, 1) // DN_CHUNK)).astype(BF16)
    hh = lax.broadcasted_iota(jnp.int32, (SPLIT, LANES, att_hd), 1)
    tgt = lax.broadcasted_iota(jnp.int32, (SPLIT, LANES, att_hd), 2)
    piece = lax.broadcasted_iota(jnp.int32, (SPLIT, LANES, att_hd), 0)
    place = ((hh < ATT_HEADS) & (tgt == LANES * (hh // 2) + SPLIT * (hh % 2) + piece)).astype(BF16)

    for i in range(depth):
        j = i // 2
        if i % 2 == 0:
            w_in = attn_w_in[j]
            qa, qb, k, va, vb, og, kc = _attn_in(
                h, vec(norm_mix_pre[i]), w_in[:, :4 * att_hd].astype(BF16),
                _pad_cols(w_in[:, 4 * att_hd:], LANES).astype(BF16), _pad_cols(vec(attn_b_forget[j]), LANES),
                vec(jnp.tile(attn_q_norm[j], ATT_HEADS)), vec(jnp.tile(attn_k_norm[j], ATT_HEADS)),
                head_sum, tri, place, lp=lp, pad=pad)
            mix = _attention(qa, qb, k, kc, va, vb, og, batch=batch, lp=lp, pad=pad)
            w_out = attn_w_out[j]
        else:
            w_in = dn_w_in[j]
            zeros = jnp.zeros((DN_HEADS,), F32)
            wba = w_in[:, 4 * dn_hd:]
            alog = jnp.concatenate([zeros, dn_a_log[j]]).astype(F32)
            dtb = jnp.concatenate([zeros, dn_dt_bias[j]]).astype(F32)
            q, k, v, og, bg, gct = _dn_in(
                h, vec(norm_mix_pre[i]), w_in[:, :4 * dn_hd].astype(BF16), dn_conv[j].astype(F32),
                _pad_cols(wba, LANES).astype(BF16), wba.T.astype(BF16),
                _pad_cols(vec(alog), LANES), _pad_cols(vec(dtb), LANES), alog.reshape(-1, 1), dtb.reshape(-1, 1),
                tri_chunk, lp=lp, pad=pad)
            mix = _deltanet(q, k, v, bg, gct, og, vec(dn_o_norm[j]), lp=lp)
            w_out = dn_w_out[j]
        h = _post_ffn(h, mix, w_out.astype(BF16), vec(norm_mix_post[i]), vec(norm_ffn_pre[i]),
                      ffn_w_up[i].astype(BF16), ffn_conv[i].astype(F32), ffn_w_down[i].astype(BF16),
                      vec(norm_ffn_post[i]), lp=lp, pad=pad)
    return h.reshape(batch, lp, d)[:, pad + N_META:]
```

```python
import functools
import math

import jax
import jax.numpy as jnp
from jax import lax
from jax.experimental import pallas as pl
from jax.experimental.pallas import tpu as pltpu

D_MODEL = 1024
N_META = 16
ATT_HEADS = 16
ATT_HEAD_DIM = 64
DN_HEADS = 8
DN_HEAD_DIM = 128
DN_CONV = 4
DN_CHUNK = 128
DN_BLOCK = 16
FFN_DIM = 2816
FFN_CONV = 3
EPS = 1e-6
NEG = -1e30
LOG2E = 1.4426950408889634

LANES = 128
ROW_TILE = 768
ATT_BLOCK = ROW_TILE
FFN_CHUNK = 256
PROJ_CHUNK = 512
CARRY_ROWS = 8
ELEM_ROWS = 64
SPLIT = 3
VMEM_LIMIT = 56 * 1024 * 1024

F32 = jnp.float32
BF16 = jnp.bfloat16


def _rmsnorm(x, g):
    return x * lax.rsqrt(jnp.mean(x * x, axis=-1, keepdims=True) + EPS) * g


def _mm(a, b):
    return jnp.dot(a, b, preferred_element_type=F32)


def _mm_nt(a, b):
    return lax.dot_general(a, b, (((1,), (1,)), ((), ())), preferred_element_type=F32)


def _mm_tn(a, b):
    return lax.dot_general(a, b, (((0,), (0,)), ((), ())), preferred_element_type=F32)


def _bmm(a, b):
    return _mm(a.astype(BF16), b.astype(BF16))


def _split3(x):
    hi = x.astype(BF16)
    r = x - hi.astype(F32)
    mid = r.astype(BF16)
    lo = (r - mid.astype(F32)).astype(BF16)
    return hi, mid, lo


def _sel_right(x, sel):
    return sum(_mm(p, sel) for p in _split3(x))


def _sel_right_t(x, sel):
    return sum(_mm_nt(p, sel) for p in _split3(x))


def _sel_left(sel, x):
    return sum(_mm(sel, p) for p in _split3(x))


def _softplus(x):
    return jnp.maximum(x, 0.0) + jnp.log1p(jnp.exp(-jnp.abs(x)))


def _sigmoid(x):
    return 1.0 / (1.0 + jnp.exp(-x))


def _row_in_batch(i, tiles_per_batch, rows):
    base = (i % tiles_per_batch) * rows
    return base + lax.broadcasted_iota(jnp.int32, (rows, 1), 0)


def _causal_conv(window, taps):
    kw = len(taps)
    y = taps[kw - 1] * window
    for s in range(1, kw):
        y = y + taps[kw - 1 - s] * pltpu.roll(window, s, 0)
    return y[CARRY_ROWS:]


def _const_spec(shape):
    nd = len(shape)
    return pl.BlockSpec(shape, lambda *_: (0,) * nd, pipeline_mode=pl.Buffered(1))


def _attn_in_kernel(h_ref, g_ref, w_ref, wf_ref, bf_ref, qg_ref, kg_ref, hsum_ref, tri_ref, place_ref,
                    qa_ref, qb_ref, k_ref, va_ref, vb_ref, og_ref, kc_ref, a_scr, y_scr, carry_scr,
                    *, tiles_per_batch, pad):
    i = pl.program_id(0)
    tm = h_ref.shape[0]
    row = _row_in_batch(i, tiles_per_batch, tm)
    a = jnp.where(row >= pad, _rmsnorm(h_ref[...], g_ref[...]), 0.0)
    a_scr[...] = a.astype(BF16)

    hd = ATT_HEADS * ATT_HEAD_DIM
    n_chunks = hd // PROJ_CHUNK
    scale = ATT_HEAD_DIM ** -0.5 * LOG2E
    even = (lax.broadcasted_iota(jnp.int32, (1, PROJ_CHUNK), 1) // ATT_HEAD_DIM) % 2 == 0
    gains = (qg_ref, kg_ref, None, None)
    chunks = [(sec, c * PROJ_CHUNK) for sec in range(4) for c in range(n_chunks)]

    def project(idx):
        sec, lo = chunks[idx]
        y_scr[idx % 2] = _mm(a_scr[...], w_ref[:, sec * hd + lo: sec * hd + lo + PROJ_CHUNK])

    def finish(idx):
        sec, lo = chunks[idx]
        cols = slice(lo, lo + PROJ_CHUNK)
        y = y_scr[idx % 2]
        if gains[sec] is not None:
            ms = _sel_right(y * y, hsum_ref[...]) * (1.0 / ATT_HEAD_DIM)
            y = y * lax.rsqrt(ms + EPS) * gains[sec][:, cols]
        if sec == 0:
            y = y * scale
            qa_ref[:, cols] = jnp.where(even, y, 0.0).astype(BF16)
            qb_ref[:, cols] = jnp.where(even, 0.0, y).astype(BF16)
        elif sec == 1:
            k_ref[:, cols] = y.astype(BF16)
        elif sec == 2:
            va_ref[:, cols] = jnp.where(even, y, 1.0).astype(BF16)
            vb_ref[:, cols] = jnp.where(even, 1.0, y).astype(BF16)
        else:
            og_ref[:, cols] = y

    project(0)
    for idx in range(len(chunks)):
        finish(idx)
        if idx + 1 < len(chunks):
            project(idx + 1)

    fl = _mm(a_scr[...], wf_ref[...]) + bf_ref[...]
    logf = -_softplus(-fl) * LOG2E

    @pl.when(i % tiles_per_batch == 0)
    def _():
        carry_scr[...] = jnp.zeros_like(carry_scr)

    c = _sel_left(tri_ref[...], logf) + carry_scr[...]
    carry_scr[...] = c[tm - 1:tm, :]
    kc = sum(_mm(p, place_ref[x]) for x, p in enumerate(_split3(-c)))
    kc_ref[...] = kc.astype(BF16)


def _attn_in(h, g, w4, wf, bfr, qg, kg, hsum, tri, place, *, lp, pad):
    t = h.shape[0]
    tm = ROW_TILE
    tpb = lp // tm
    hd = ATT_HEADS * ATT_HEAD_DIM
    row_spec = pl.BlockSpec((tm, D_MODEL), lambda i: (i, 0))
    out_shapes = (
        jax.ShapeDtypeStruct((t, hd), BF16),
        jax.ShapeDtypeStruct((t, hd), BF16),
        jax.ShapeDtypeStruct((t, hd), BF16),
        jax.ShapeDtypeStruct((t, hd), BF16),
        jax.ShapeDtypeStruct((t, hd), BF16),
        jax.ShapeDtypeStruct((t, hd), F32),
        jax.ShapeDtypeStruct((t, hd), BF16),
    )
    consts = (g, w4, wf, bfr, qg, kg, hsum, tri, place)
    return pl.pallas_call(
        functools.partial(_attn_in_kernel, tiles_per_batch=tpb, pad=pad),
        grid=(t // tm,),
        in_specs=[row_spec] + [_const_spec(c.shape) for c in consts],
        out_specs=(row_spec,) * 7,
        out_shape=out_shapes,
        scratch_shapes=[pltpu.VMEM((tm, D_MODEL), BF16), pltpu.VMEM((2, tm, PROJ_CHUNK), F32),
                        pltpu.VMEM((1, LANES), F32)],
        compiler_params=pltpu.CompilerParams(dimension_semantics=("arbitrary",), vmem_limit_bytes=VMEM_LIMIT),
        name="attn_in",
    )(h, *consts)


def _attn_kernel(qa_ref, qb_ref, k_ref, kc_ref, va_ref, vb_ref, og_ref, o_ref,
                 s_scr, p_scr, alpha_scr, m_scr, acc_scr, *, pad, n_blocks):
    tq = ATT_BLOCK
    half = ATT_HEAD_DIM
    n_steps = n_blocks * (n_blocks + 1) // 2
    lane = lax.broadcasted_iota(jnp.int32, (tq, LANES), 1)
    ones_sel = [jnp.where((lane >= SPLIT * e) & (lane < SPLIT * (e + 1)), 1.0, 0.0).astype(BF16) for e in range(2)]
    m_scr[...] = jnp.full(m_scr.shape, NEG, F32)
    acc_scr[...] = jnp.zeros(acc_scr.shape, F32)

    def rows(b):
        return pl.ds(pl.multiple_of(b * tq, tq), tq)

    def advance(ij):
        i, j = ij
        return jnp.where(j == i, i + 1, i), jnp.where(j == i, 0, j + 1)

    def logits(ij, slot):
        i, j = ij
        k_aug = jnp.concatenate([k_ref[0, rows(j), :], kc_ref[0, rows(j), :]], axis=1)
        for e, q_ref in enumerate((qa_ref, qb_ref)):
            s_scr[slot, e] = _mm_nt(jnp.concatenate([q_ref[0, rows(i), :], ones_sel[e]], axis=1), k_aug)

    def softmax(ij, slot, masked):
        i, j = ij
        if masked:
            kpos = j * tq + lax.broadcasted_iota(jnp.int32, (1, tq), 1)
            kpos = jnp.where(kpos < pad, jnp.int32(2 ** 30), kpos)
            mask = kpos <= i * tq + lax.broadcasted_iota(jnp.int32, (tq, 1), 0)
        for e in range(2):
            s = s_scr[slot, e]
            if masked:
                s = jnp.where(mask, s, NEG)
            m_old = jnp.where(j == 0, NEG, m_scr[e])
            m_new = jnp.maximum(m_old, jnp.max(s, axis=-1, keepdims=True))
            p_scr[slot, :, e * tq:(e + 1) * tq] = jnp.exp2(s - m_new).astype(BF16)
            alpha_scr[slot, :, e * LANES:(e + 1) * LANES] = jnp.broadcast_to(jnp.exp2(m_old - m_new), (tq, LANES))
            m_scr[e] = m_new

    def values(ij, slot):
        _, j = ij
        nil = jnp.zeros((tq, LANES), BF16)
        vals = jnp.concatenate([jnp.concatenate([va_ref[0, rows(j), :], nil], axis=1),
                                jnp.concatenate([nil, vb_ref[0, rows(j), :]], axis=1)], axis=0)
        acc_scr[...] = alpha_scr[slot] * acc_scr[...] + _mm(p_scr[slot], vals)

    def finalize(i):
        acc = acc_scr[...]
        outs = [acc[:, e * LANES:(e + 1) * LANES] for e in range(2)]
        outs = [x / pltpu.roll(x, half, 1) for x in outs]
        o = jnp.where(lane < half, outs[0], outs[1])
        o_ref[0, rows(i), :] = (o * _sigmoid(og_ref[0, rows(i), :])).astype(o_ref.dtype)

    def finalize_if_last(ij):
        i, j = ij

        @pl.when(j == i)
        def _():
            finalize(i)

    def region(prev, cur, nxt, slot):
        i, j = cur
        needs_mask = (j == 0) | (j == i)
        for masked, taken in ((True, needs_mask), (False, jnp.logical_not(needs_mask))):
            @pl.when(taken)
            def _():
                values(prev, 1 - slot)
                softmax(cur, slot, masked)
                logits(nxt, 1 - slot)
        finalize_if_last(prev)

    first = (jnp.int32(0), jnp.int32(0))
    logits(first, 0)
    if n_steps == 1:
        softmax(first, 0, True)
        values(first, 0)
    else:
        second = advance(first)
        softmax(first, 0, True)
        logits(second, 1)

        def two_steps(_, carry):
            prev, cur = carry
            nxt = advance(cur)
            region(prev, cur, nxt, 1)
            nxt2 = advance(nxt)
            region(cur, nxt, nxt2, 0)
            return nxt, nxt2

        n_mid = n_steps - 2
        prev, cur = lax.fori_loop(0, n_mid // 2, two_steps, (first, second))
        if n_mid % 2:
            nxt = advance(cur)
            region(prev, cur, nxt, 1)
            prev, cur = cur, nxt
        slot = (n_steps - 1) % 2
        values(prev, 1 - slot)
        softmax(cur, slot, True)
        finalize_if_last(prev)
        values(cur, slot)
    finalize(n_blocks - 1)


def _attention(qa, qb, k, kc, va, vb, og, *, batch, lp, pad):
    t, hd = qa.shape
    tq = ATT_BLOCK
    n_pairs = hd // LANES
    spec = pl.BlockSpec((1, lp, LANES), lambda b, p: (b, 0, p))
    once = pl.BlockSpec((1, lp, LANES), lambda b, p: (b, 0, p), pipeline_mode=pl.Buffered(1))
    to3 = lambda z: z.reshape(batch, lp, hd)
    out = pl.pallas_call(
        functools.partial(_attn_kernel, pad=pad, n_blocks=lp // tq),
        grid=(batch, n_pairs),
        in_specs=[spec, spec, once, once, once, once, once],
        out_specs=spec,
        out_shape=jax.ShapeDtypeStruct((batch, lp, hd), BF16),
        scratch_shapes=[pltpu.VMEM((2, 2, tq, tq), F32), pltpu.VMEM((2, tq, 2 * tq), BF16),
                        pltpu.VMEM((2, tq, 2 * LANES), F32), pltpu.VMEM((2, tq, 1), F32),
                        pltpu.VMEM((tq, 2 * LANES), F32)],
        compiler_params=pltpu.CompilerParams(
            dimension_semantics=("parallel", "parallel"), vmem_limit_bytes=VMEM_LIMIT),
        name="fox_attention",
    )(to3(qa), to3(qb), to3(k), to3(kc), to3(va), to3(vb), to3(og))
    return out.reshape(t, hd)


def _post_ffn_kernel(h_ref, mix_ref, wo_ref, gpost_ref, gpre_ref, wup_ref, cw_ref, wdn_ref, gfpost_ref,
                     out_ref, a_scr, u_scr, act_scr, acc_scr, carry_scr, *, tiles_per_batch, pad):
    i = pl.program_id(0)
    tm = h_ref.shape[0]
    row = _row_in_batch(i, tiles_per_batch, tm)
    m = _mm(mix_ref[...], wo_ref[...])
    h1 = h_ref[...] + _rmsnorm(m, gpost_ref[...])
    out_ref[...] = h1
    a = jnp.where(row >= pad, _rmsnorm(h1, gpre_ref[...]), 0.0)
    a_scr[...] = a.astype(BF16)

    @pl.when(i % tiles_per_batch == 0)
    def _():
        carry_scr[...] = jnp.zeros_like(carry_scr)

    fc = FFN_CHUNK
    n_chunks = FFN_DIM // fc

    def project(c):
        for half, base in enumerate((0, FFN_DIM)):
            cols = slice(base + c * fc, base + (c + 1) * fc)
            u_scr[c % 2, half, :CARRY_ROWS, :] = carry_scr[:, cols]
            u_scr[c % 2, half, CARRY_ROWS:, :] = _mm(a_scr[...], wup_ref[:, cols])

    def activate(c):
        taps = [[cw_ref[s:s + 1, base + c * fc:base + (c + 1) * fc] for s in range(FFN_CONV)]
                for base in (0, FFN_DIM)]
        for r in range(0, tm, ELEM_ROWS):
            gate, up = [_causal_conv(u_scr[c % 2, half, r:r + CARRY_ROWS + ELEM_ROWS, :], taps[half])
                        for half in range(2)]
            gelu = 0.5 * gate * (1.0 + jnp.tanh(math.sqrt(2.0 / math.pi) * (gate + 0.044715 * (gate * gate * gate))))
            act_scr[c % 2, r:r + ELEM_ROWS, :] = (gelu * up).astype(BF16)
        for half, base in enumerate((0, FFN_DIM)):
            carry_scr[:, base + c * fc:base + (c + 1) * fc] = u_scr[c % 2, half, tm:, :]

    def contract(c):
        part = _mm(act_scr[c % 2], wdn_ref[c * fc:(c + 1) * fc, :])
        if c == 0:
            acc_scr[...] = part
        else:
            acc_scr[...] += part

    project(0)
    for c in range(n_chunks + 1):
        if c >= 1:
            contract(c - 1)
        if c < n_chunks:
            activate(c)
        if c + 1 < n_chunks:
            project(c + 1)
    out_ref[...] = out_ref[...] + _rmsnorm(acc_scr[...], gfpost_ref[...])


def _post_ffn(h, mix, wo, gpost, gpre, wup, cw, wdn, gfpost, *, lp, pad):
    t = h.shape[0]
    tm = ROW_TILE
    tpb = lp // tm
    row_spec = pl.BlockSpec((tm, D_MODEL), lambda i: (i, 0))
    vec = _const_spec((1, D_MODEL))
    return pl.pallas_call(
        functools.partial(_post_ffn_kernel, tiles_per_batch=tpb, pad=pad),
        grid=(t // tm,),
        in_specs=[row_spec, row_spec, _const_spec(wo.shape), vec, vec, _const_spec(wup.shape),
                  _const_spec(cw.shape), _const_spec(wdn.shape), vec],
        out_specs=row_spec,
        out_shape=jax.ShapeDtypeStruct((t, D_MODEL), F32),
        scratch_shapes=[pltpu.VMEM((tm, D_MODEL), BF16), pltpu.VMEM((2, 2, CARRY_ROWS + tm, FFN_CHUNK), F32),
                        pltpu.VMEM((2, tm, FFN_CHUNK), BF16), pltpu.VMEM((tm, D_MODEL), F32),
                        pltpu.VMEM((CARRY_ROWS, 2 * FFN_DIM), F32)],
        compiler_params=pltpu.CompilerParams(dimension_semantics=("arbitrary",), vmem_limit_bytes=VMEM_LIMIT),
        name="post_ffn",
    )(h, mix, wo, gpost, gpre, wup, cw, wdn, gfpost)


def _dn_in_kernel(h_ref, g_ref, w_ref, cw_ref, wba_ref, wbat_ref, alog_ref, dt_ref, alogt_ref, dtt_ref, tri_ref,
                  q_ref, k_ref, v_ref, og_ref, bg_ref, gct_ref, a_scr, u_scr, carry_scr, *, tiles_per_batch, pad):
    i = pl.program_id(0)
    tm = h_ref.shape[0]
    row = _row_in_batch(i, tiles_per_batch, tm)
    a = jnp.where(row >= pad, _rmsnorm(h_ref[...], g_ref[...]), 0.0)
    a_scr[...] = a.astype(BF16)

    @pl.when(i % tiles_per_batch == 0)
    def _():
        carry_scr[...] = jnp.zeros_like(carry_scr)

    hd = DN_HEADS * DN_HEAD_DIM
    n_chunks = hd // PROJ_CHUNK
    heads_per_chunk = PROJ_CHUNK // DN_HEAD_DIM
    out_refs = (q_ref, k_ref, v_ref, og_ref)
    chunks = [(sec, c * PROJ_CHUNK) for sec in range(4) for c in range(n_chunks)]

    def project(idx):
        sec, lo = chunks[idx]
        cols = slice(sec * hd + lo, sec * hd + lo + PROJ_CHUNK)
        if sec < 3:
            u_scr[idx % 2, :CARRY_ROWS, :] = carry_scr[:, cols]
        u_scr[idx % 2, CARRY_ROWS:, :] = _mm(a_scr[...], w_ref[:, cols])

    def finish(idx):
        sec, lo = chunks[idx]
        if sec == 3:
            og_ref[:, lo:lo + PROJ_CHUNK] = u_scr[idx % 2, CARRY_ROWS:, :]
            return
        cols = slice(sec * hd + lo, sec * hd + lo + PROJ_CHUNK)
        taps = [cw_ref[s:s + 1, cols] for s in range(DN_CONV)]
        for r in range(0, tm, ELEM_ROWS):
            y = _causal_conv(u_scr[idx % 2, r:r + CARRY_ROWS + ELEM_ROWS, :], taps)
            y = y * _sigmoid(y)
            for hh in range(heads_per_chunk):
                yh = y[:, hh * DN_HEAD_DIM:(hh + 1) * DN_HEAD_DIM]
                if sec < 2:
                    yh = yh * lax.rsqrt(jnp.sum(yh * yh, axis=-1, keepdims=True) + EPS)
                    if sec == 0:
                        yh = yh * (DN_HEAD_DIM ** -0.5)
                col = lo + hh * DN_HEAD_DIM
                out_refs[sec][r:r + ELEM_ROWS, col:col + DN_HEAD_DIM] = yh.astype(BF16)
        carry_scr[:, cols] = u_scr[idx % 2, tm:, :]

    project(0)
    for idx in range(len(chunks)):
        finish(idx)
        if idx + 1 < len(chunks):
            project(idx + 1)

    ba = _mm(a_scr[...], wba_ref[...])
    lane = lax.broadcasted_iota(jnp.int32, (1, LANES), 1)
    beta = _sigmoid(ba)
    g = -jnp.exp(alog_ref[...]) * _softplus(ba + dt_ref[...])
    bg_ref[...] = jnp.where(lane < DN_HEADS, beta, _sel_left(tri_ref[...], g))
    bat = _mm_nt(wbat_ref[...], a_scr[...])
    gt = -jnp.exp(alogt_ref[...]) * _softplus(bat + dtt_ref[...])
    gct = _sel_right_t(gt, tri_ref[...])
    for c in range(tm // DN_CHUNK):
        gct_ref[c] = gct[:, c * DN_CHUNK:(c + 1) * DN_CHUNK]


def _dn_in(h, g, w4, cw, wba, wbat, alog, dtb, alogt, dtbt, tri, *, lp, pad):
    t = h.shape[0]
    tm = ROW_TILE
    tpb = lp // tm
    hd = DN_HEADS * DN_HEAD_DIM
    row_spec = pl.BlockSpec((tm, D_MODEL), lambda i: (i, 0))
    out_shapes = (
        jax.ShapeDtypeStruct((t, hd), BF16),
        jax.ShapeDtypeStruct((t, hd), BF16),
        jax.ShapeDtypeStruct((t, hd), BF16),
        jax.ShapeDtypeStruct((t, hd), F32),
        jax.ShapeDtypeStruct((t, LANES), F32),
        jax.ShapeDtypeStruct((t // DN_CHUNK, 2 * DN_HEADS, DN_CHUNK), F32),
    )
    consts = (g, w4, cw, wba, wbat, alog, dtb, alogt, dtbt, tri)
    return pl.pallas_call(
        functools.partial(_dn_in_kernel, tiles_per_batch=tpb, pad=pad),
        grid=(t // tm,),
        in_specs=[row_spec] + [_const_spec(c.shape) for c in consts],
        out_specs=(row_spec, row_spec, row_spec, row_spec, pl.BlockSpec((tm, LANES), lambda i: (i, 0)),
                   pl.BlockSpec((tm // DN_CHUNK, 2 * DN_HEADS, DN_CHUNK), lambda i: (i, 0, 0))),
        out_shape=out_shapes,
        scratch_shapes=[pltpu.VMEM((tm, D_MODEL), BF16), pltpu.VMEM((2, CARRY_ROWS + tm, PROJ_CHUNK), F32),
                        pltpu.VMEM((CARRY_ROWS, 3 * hd), F32)],
        compiler_params=pltpu.CompilerParams(dimension_semantics=("arbitrary",), vmem_limit_bytes=VMEM_LIMIT),
        name="dn_in",
    )(h, *consts)


def _dn_kernel(q_ref, k_ref, v_ref, bg_ref, gct_ref, og_ref, gain_ref, o_ref, s_scr, *, tiles_per_batch):
    i = pl.program_id(0)
    tm = q_ref.shape[0]
    cs = DN_CHUNK
    dh = DN_HEAD_DIM
    heads = range(DN_HEADS)

    @pl.when(i % tiles_per_batch == 0)
    def _():
        s_scr[...] = jnp.zeros_like(s_scr)

    ri = lax.broadcasted_iota(jnp.int32, (cs, cs), 0)
    ci = lax.broadcasted_iota(jnp.int32, (cs, cs), 1)
    incl = ri >= ci
    strict = ri > ci
    blk = (ri // DN_BLOCK) == (ci // DN_BLOCK)

    def chunk(cidx, _):
        r0 = pl.multiple_of(cidx * cs, cs)
        rows = pl.ds(r0, cs)
        bg = bg_ref[rows, :]
        gct = gct_ref[cidx]
        cols = [slice(h * dh, (h + 1) * dh) for h in heads]
        kb = [k_ref[rows, cols[h]] for h in heads]
        qb = [q_ref[rows, cols[h]] for h in heads]
        beta = [bg[:, h:h + 1] for h in heads]
        gc = [bg[:, DN_HEADS + h:DN_HEADS + h + 1] for h in heads]
        dec = [jnp.where(incl, jnp.exp(gc[h] - gct[DN_HEADS + h:DN_HEADS + h + 1, :]), 0.0) for h in heads]
        a_mat = [jnp.where(strict, _mm_nt(kb[h], kb[h]) * dec[h] * beta[h], 0.0) for h in heads]
        off = [jnp.where(blk, 0.0, a_mat[h]) for h in heads]
        mk = [jnp.where(blk, -a_mat[h], 0.0) for h in heads]
        nk = mk
        for _ in range(3):
            mk = [_bmm(mk[h], mk[h]) for h in heads]
            nk = [nk[h] + mk[h] + _bmm(nk[h], mk[h]) for h in heads]
        b = [off[h] + _bmm(nk[h], off[h]) for h in heads]
        qn = [-b[h] for h in heads]
        bp = b
        for _ in range(2):
            bp = [_bmm(bp[h], bp[h]) for h in heads]
            qn = [qn[h] + bp[h] + _bmm(qn[h], bp[h]) for h in heads]
        nt = [qn[h] + nk[h] + _bmm(qn[h], nk[h]) for h in heads]
        egc = [jnp.exp(gc[h]) for h in heads]
        rhs = [jnp.concatenate([kb[h].astype(F32) * (beta[h] * egc[h]),
                                v_ref[rows, cols[h]].astype(F32) * beta[h]], axis=-1) for h in heads]
        wu = [rhs[h] + _bmm(nt[h], rhs[h]) for h in heads]
        qk = [jnp.where(incl, _mm_nt(qb[h], kb[h]) * dec[h], 0.0).astype(BF16) for h in heads]
        s_b = [s_scr[h].astype(BF16) for h in heads]
        u = [wu[h][:, dh:] - _bmm(wu[h][:, :dh], s_b[h]) for h in heads]
        ub = [u[h].astype(BF16) for h in heads]
        o = [_bmm(qb[h].astype(F32) * egc[h], s_b[h]) + _mm(qk[h], ub[h]) for h in heads]
        g_last = [gc[h][cs - 1:cs, :] for h in heads]
        kd = [(kb[h].astype(F32) * jnp.exp(g_last[h] - gc[h])).astype(BF16) for h in heads]
        for h in heads:
            s_scr[h] = jnp.exp(g_last[h]) * s_scr[h] + _mm_tn(kd[h], ub[h])
            og = og_ref[rows, cols[h]]
            on = _rmsnorm(o[h], gain_ref[...]) * (og * _sigmoid(og))
            o_ref[rows, cols[h]] = on.astype(o_ref.dtype)
        return 0

    lax.fori_loop(0, tm // cs, chunk, 0)


def _deltanet(q, k, v, bg, gct, og, gain, *, lp):
    t, hd = q.shape
    tm = ROW_TILE
    tpb = lp // tm
    row_spec = pl.BlockSpec((tm, hd), lambda i: (i, 0))
    return pl.pallas_call(
        functools.partial(_dn_kernel, tiles_per_batch=tpb),
        grid=(t // tm,),
        in_specs=[row_spec, row_spec, row_spec, pl.BlockSpec((tm, LANES), lambda i: (i, 0)),
                  pl.BlockSpec((tm // DN_CHUNK, 2 * DN_HEADS, DN_CHUNK), lambda i: (i, 0, 0)), row_spec,
                  _const_spec((1, DN_HEAD_DIM))],
        out_specs=row_spec,
        out_shape=jax.ShapeDtypeStruct((t, hd), BF16),
        scratch_shapes=[pltpu.VMEM((DN_HEADS, DN_HEAD_DIM, DN_HEAD_DIM), F32)],
        compiler_params=pltpu.CompilerParams(dimension_semantics=("arbitrary",), vmem_limit_bytes=VMEM_LIMIT),
        name="gated_deltanet",
    )(q, k, v, bg, gct, og, gain)


def _pad_cols(w, n):
    return jnp.pad(w, ((0, 0), (0, n - w.shape[1])))


def kernel(x, meta_tokens, norm_mix_pre, norm_mix_post, norm_ffn_pre, norm_ffn_post, attn_w_in, attn_b_forget, attn_q_norm, attn_k_norm, attn_w_out, dn_w_in, dn_conv, dn_a_log, dn_dt_bias, dn_o_norm, dn_w_out, ffn_w_up, ffn_conv, ffn_w_down):
    batch, seq, d = x.shape
    assert d == D_MODEL
    depth = norm_mix_pre.shape[0]
    length = N_META + seq
    lp = -(-length // ROW_TILE) * ROW_TILE
    pad = lp - length

    meta = jnp.broadcast_to(meta_tokens[None].astype(x.dtype), (batch, N_META, d))
    h = jnp.concatenate([jnp.zeros((batch, pad, d), x.dtype), meta, x], axis=1).reshape(batch * lp, d)

    att_hd = ATT_HEADS * ATT_HEAD_DIM
    dn_hd = DN_HEADS * DN_HEAD_DIM
    vec = lambda g: g.reshape(1, -1).astype(F32)
    iota = lambda n, ax: lax.broadcasted_iota(jnp.int32, (n, n), ax)
    head_sum = (iota(PROJ_CHUNK, 0) // ATT_HEAD_DIM == iota(PROJ_CHUNK, 1) // ATT_HEAD_DIM).astype(BF16)
    lower = iota(ROW_TILE, 0) >= iota(ROW_TILE, 1)
    tri = lower.astype(BF16)
    tri_chunk = (lower & (iota(ROW_TILE, 0) // DN_CHUNK == iota(ROW_TILE, 1) // DN_CHUNK)).astype(BF16)
    hh = lax.broadcasted_iota(jnp.int32, (SPLIT, LANES, att_hd), 1)
    tgt = lax.broadcasted_iota(jnp.int32, (SPLIT, LANES, att_hd), 2)
    piece = lax.broadcasted_iota(jnp.int32, (SPLIT, LANES, att_hd), 0)
    place = ((hh < ATT_HEADS) & (tgt == LANES * (hh // 2) + SPLIT * (hh % 2) + piece)).astype(BF16)

    for i in range(depth):
        j = i // 2
        if i % 2 == 0:
            w_in = attn_w_in[j]
            qa, qb, k, va, vb, og, kc = _attn_in(
                h, vec(norm_mix_pre[i]), w_in[:, :4 * att_hd].astype(BF16),
                _pad_cols(w_in[:, 4 * att_hd:], LANES).astype(BF16), _pad_cols(vec(attn_b_forget[j]), LANES),
                vec(jnp.tile(attn_q_norm[j], ATT_HEADS)), vec(jnp.tile(attn_k_norm[j], ATT_HEADS)),
                head_sum, tri, place, lp=lp, pad=pad)
            mix = _attention(qa, qb, k, kc, va, vb, og, batch=batch, lp=lp, pad=pad)
            w_out = attn_w_out[j]
        else:
            w_in = dn_w_in[j]
            zeros = jnp.zeros((DN_HEADS,), F32)
            wba = w_in[:, 4 * dn_hd:]
            alog = jnp.concatenate([zeros, dn_a_log[j]]).astype(F32)
            dtb = jnp.concatenate([zeros, dn_dt_bias[j]]).astype(F32)
            q, k, v, og, bg, gct = _dn_in(
                h, vec(norm_mix_pre[i]), w_in[:, :4 * dn_hd].astype(BF16), dn_conv[j].astype(F32),
                _pad_cols(wba, LANES).astype(BF16), wba.T.astype(BF16),
                _pad_cols(vec(alog), LANES), _pad_cols(vec(dtb), LANES), alog.reshape(-1, 1), dtb.reshape(-1, 1),
                tri_chunk, lp=lp, pad=pad)
            mix = _deltanet(q, k, v, bg, gct, og, vec(dn_o_norm[j]), lp=lp)
            w_out = dn_w_out[j]
        h = _post_ffn(h, mix, w_out.astype(BF16), vec(norm_mix_post[i]), vec(norm_ffn_pre[i]),
                      ffn_w_up[i].astype(BF16), ffn_conv[i].astype(F32), ffn_w_down[i].astype(BF16),
                      vec(norm_ffn_post[i]), lp=lp, pad=pad)
    return h.reshape(batch, lp, d)[:, pad + N_META:]
```

```python
import functools
import math

import jax
import jax.numpy as jnp
from jax import lax
from jax.experimental import pallas as pl
from jax.experimental.pallas import tpu as pltpu

D_MODEL = 1024
N_META = 16
ATT_HEADS = 16
ATT_HEAD_DIM = 64
DN_HEADS = 8
DN_HEAD_DIM = 128
DN_CONV = 4
DN_CHUNK = 128
DN_BLOCK = 16
FFN_DIM = 2816
FFN_CONV = 3
EPS = 1e-6
NEG = -1e30
LOG2E = 1.4426950408889634

LANES = 128
ROW_TILE = 768
ATT_BLOCK = ROW_TILE
FFN_CHUNK = 256
PROJ_CHUNK = 512
CARRY_ROWS = 8
ELEM_ROWS = 64
SOFTMAX_ROWS = 32
SPLIT = 3
KC_LANES = 8
VMEM_LIMIT = 56 * 1024 * 1024

F32 = jnp.float32
BF16 = jnp.bfloat16


def _rmsnorm(x, g):
    return x * lax.rsqrt(jnp.mean(x * x, axis=-1, keepdims=True) + EPS) * g


def _mm(a, b):
    return jnp.dot(a, b, preferred_element_type=F32)


def _mm_nt(a, b):
    return lax.dot_general(a, b, (((1,), (1,)), ((), ())), preferred_element_type=F32)


def _mm_tn(a, b):
    return lax.dot_general(a, b, (((0,), (0,)), ((), ())), preferred_element_type=F32)


def _bmm(a, b):
    return _mm(a.astype(BF16), b.astype(BF16))


def _split3(x, pieces=SPLIT):
    out = []
    for _ in range(pieces - 1):
        out.append(x.astype(BF16))
        x = x - out[-1].astype(F32)
    out.append(x.astype(BF16))
    return out


def _sel_right(x, sel, pieces=SPLIT):
    return sum(_mm(p, sel) for p in _split3(x, pieces))


def _sel_right_t(x, sel):
    return sum(_mm_nt(p, sel) for p in _split3(x))


def _sel_left(sel, x):
    return sum(_mm(sel, p) for p in _split3(x))


def _softplus(x):
    return jnp.maximum(x, 0.0) + jnp.log1p(jnp.exp(-jnp.abs(x)))


def _sigmoid(x):
    return 1.0 / (1.0 + jnp.exp(-x))


def _row_in_batch(i, tiles_per_batch, rows):
    base = (i % tiles_per_batch) * rows
    return base + lax.broadcasted_iota(jnp.int32, (rows, 1), 0)


def _causal_conv(window, taps):
    kw = len(taps)
    y = taps[kw - 1] * window
    for s in range(1, kw):
        y = y + taps[kw - 1 - s] * pltpu.roll(window, s, 0)
    return y[CARRY_ROWS:]


def _const_spec(shape):
    nd = len(shape)
    return pl.BlockSpec(shape, lambda *_: (0,) * nd, pipeline_mode=pl.Buffered(1))


def _attn_in_kernel(h_ref, g_ref, w_ref, wf_ref, bf_ref, qg_ref, kg_ref, hsum_ref, tri_ref, place_ref,
                    qa_ref, qb_ref, k_ref, va_ref, vb_ref, og_ref, kc_ref, a_scr, y_scr, carry_scr,
                    *, tiles_per_batch, pad):
    i = pl.program_id(0)
    tm = h_ref.shape[0]
    row = _row_in_batch(i, tiles_per_batch, tm)
    a = jnp.where(row >= pad, _rmsnorm(h_ref[...], g_ref[...]), 0.0)
    a_scr[...] = a.astype(BF16)

    hd = ATT_HEADS * ATT_HEAD_DIM
    n_chunks = hd // PROJ_CHUNK
    scale = ATT_HEAD_DIM ** -0.5 * LOG2E
    even = (lax.broadcasted_iota(jnp.int32, (1, PROJ_CHUNK), 1) // ATT_HEAD_DIM) % 2 == 0
    gains = (qg_ref, kg_ref, None, None)
    chunks = [(sec, c * PROJ_CHUNK) for sec in range(4) for c in range(n_chunks)]

    def project(idx):
        sec, lo = chunks[idx]
        y_scr[idx % 2] = _mm(a_scr[...], w_ref[:, sec * hd + lo: sec * hd + lo + PROJ_CHUNK])

    def finish(idx):
        sec, lo = chunks[idx]
        cols = slice(lo, lo + PROJ_CHUNK)
        y = y_scr[idx % 2]
        if gains[sec] is not None:
            ms = _sel_right(y * y, hsum_ref[...], pieces=2) * (1.0 / ATT_HEAD_DIM)
            y = y * lax.rsqrt(ms + EPS) * gains[sec][:, cols]
        if sec == 0:
            y = y * scale
            qa_ref[:, cols] = jnp.where(even, y, 0.0).astype(BF16)
            qb_ref[:, cols] = jnp.where(even, 0.0, y).astype(BF16)
        elif sec == 1:
            k_ref[:, cols] = y.astype(BF16)
        elif sec == 2:
            va_ref[:, cols] = jnp.where(even, y, 1.0).astype(BF16)
            vb_ref[:, cols] = jnp.where(even, 1.0, y).astype(BF16)
        else:
            og_ref[:, cols] = y

    project(0)
    for idx in range(len(chunks)):
        finish(idx)
        if idx + 1 < len(chunks):
            project(idx + 1)

    fl = _mm(a_scr[...], wf_ref[...]) + bf_ref[...]
    logf = -_softplus(-fl) * LOG2E

    @pl.when(i % tiles_per_batch == 0)
    def _():
        carry_scr[...] = jnp.zeros_like(carry_scr)

    c = _sel_left(tri_ref[...], logf) + carry_scr[...]
    carry_scr[...] = c[tm - 1:tm, :]
    kc = sum(_mm(p, place_ref[x]) for x, p in enumerate(_split3(-c)))
    kc_ref[...] = kc.astype(BF16)


def _attn_in(h, g, w4, wf, bfr, qg, kg, hsum, tri, place, *, lp, pad):
    t = h.shape[0]
    tm = ROW_TILE
    tpb = lp // tm
    hd = ATT_HEADS * ATT_HEAD_DIM
    row_spec = pl.BlockSpec((tm, D_MODEL), lambda i: (i, 0))
    out_shapes = (
        jax.ShapeDtypeStruct((t, hd), BF16),
        jax.ShapeDtypeStruct((t, hd), BF16),
        jax.ShapeDtypeStruct((t, hd), BF16),
        jax.ShapeDtypeStruct((t, hd), BF16),
        jax.ShapeDtypeStruct((t, hd), BF16),
        jax.ShapeDtypeStruct((t, hd), F32),
        jax.ShapeDtypeStruct((t, LANES), BF16),
    )
    consts = (g, w4, wf, bfr, qg, kg, hsum, tri, place)
    return pl.pallas_call(
        functools.partial(_attn_in_kernel, tiles_per_batch=tpb, pad=pad),
        grid=(t // tm,),
        in_specs=[row_spec] + [_const_spec(c.shape) for c in consts],
        out_specs=(row_spec,) * 6 + (pl.BlockSpec((tm, LANES), lambda i: (i, 0)),),
        out_shape=out_shapes,
        scratch_shapes=[pltpu.VMEM((tm, D_MODEL), BF16), pltpu.VMEM((2, tm, PROJ_CHUNK), F32),
                        pltpu.VMEM((1, LANES), F32)],
        compiler_params=pltpu.CompilerParams(dimension_semantics=("arbitrary",), vmem_limit_bytes=VMEM_LIMIT),
        name="attn_in",
    )(h, *consts)


def _attn_kernel(qa_ref, qb_ref, k_ref, kc_ref, va_ref, vb_ref, og_ref, o_ref,
                 s_scr, p_scr, alpha_scr, m_scr, acc_scr, *, pad, n_blocks):
    tq = ATT_BLOCK
    half = ATT_HEAD_DIM
    n_steps = n_blocks * (n_blocks + 1) // 2
    lane = lax.broadcasted_iota(jnp.int32, (tq, LANES), 1)
    first_lane = KC_LANES * pl.program_id(1)
    ones_sel = [jnp.where((lane >= first_lane + SPLIT * e) & (lane < first_lane + SPLIT * (e + 1)), 1.0, 0.0)
                .astype(BF16) for e in range(2)]
    m_scr[...] = jnp.full(m_scr.shape, NEG, F32)
    acc_scr[...] = jnp.zeros(acc_scr.shape, F32)

    def rows(b):
        return pl.ds(pl.multiple_of(b * tq, tq), tq)

    def advance(ij):
        i, j = ij
        return jnp.where(j == i, i + 1, i), jnp.where(j == i, 0, j + 1)

    def logits(ij, slot):
        i, j = ij
        k_aug = jnp.concatenate([k_ref[0, rows(j), :], kc_ref[0, rows(j), :]], axis=1)
        for e, q_ref in enumerate((qa_ref, qb_ref)):
            s_scr[slot, e] = _mm_nt(jnp.concatenate([q_ref[0, rows(i), :], ones_sel[e]], axis=1), k_aug)

    def softmax(ij, slot, masked):
        i, j = ij
        if masked:
            kpos = j * tq + lax.broadcasted_iota(jnp.int32, (1, tq), 1)
            kpos = jnp.where(kpos < pad, jnp.int32(2 ** 30), kpos)
        for r in range(0, tq, SOFTMAX_ROWS):
            rs = slice(r, r + SOFTMAX_ROWS)
            if masked:
                mask = kpos <= i * tq + r + lax.broadcasted_iota(jnp.int32, (SOFTMAX_ROWS, 1), 0)
            for e in range(2):
                s = s_scr[slot, e, rs, :]
                if masked:
                    s = jnp.where(mask, s, NEG)
                m_old = jnp.where(j == 0, NEG, m_scr[e, rs, :])
                m_new = jnp.maximum(m_old, jnp.max(s, axis=-1, keepdims=True))
                p_scr[slot, rs, e * tq:(e + 1) * tq] = jnp.exp2(s - m_new).astype(BF16)
                alpha_scr[slot, rs, e * LANES:(e + 1) * LANES] = jnp.broadcast_to(
                    jnp.exp2(m_old - m_new), (SOFTMAX_ROWS, LANES))
                m_scr[e, rs, :] = m_new

    def values(ij, slot):
        _, j = ij
        nil = jnp.zeros((tq, LANES), BF16)
        vals = jnp.concatenate([jnp.concatenate([va_ref[0, rows(j), :], nil], axis=1),
                                jnp.concatenate([nil, vb_ref[0, rows(j), :]], axis=1)], axis=0)
        acc_scr[...] = alpha_scr[slot] * acc_scr[...] + _mm(p_scr[slot], vals)

    def finalize(i):
        acc = acc_scr[...]
        outs = [acc[:, e * LANES:(e + 1) * LANES] for e in range(2)]
        outs = [x / pltpu.roll(x, half, 1) for x in outs]
        o = jnp.where(lane < half, outs[0], outs[1])
        o_ref[0, rows(i), :] = (o * _sigmoid(og_ref[0, rows(i), :])).astype(o_ref.dtype)

    def finalize_if_last(ij):
        i, j = ij

        @pl.when(j == i)
        def _():
            finalize(i)

    def region(prev, cur, nxt, slot):
        i, j = cur
        needs_mask = (j == 0) | (j == i)
        for masked, taken in ((True, needs_mask), (False, jnp.logical_not(needs_mask))):
            @pl.when(taken)
            def _():
                values(prev, 1 - slot)
                softmax(cur, slot, masked)
                logits(nxt, 1 - slot)
        finalize_if_last(prev)

    first = (jnp.int32(0), jnp.int32(0))
    logits(first, 0)
    if n_steps == 1:
        softmax(first, 0, True)
        values(first, 0)
    else:
        second = advance(first)
        softmax(first, 0, True)
        logits(second, 1)

        def two_steps(_, carry):
            prev, cur = carry
            nxt = advance(cur)
            region(prev, cur, nxt, 1)
            nxt2 = advance(nxt)
            region(cur, nxt, nxt2, 0)
            return nxt, nxt2

        n_mid = n_steps - 2
        prev, cur = lax.fori_loop(0, n_mid // 2, two_steps, (first, second))
        if n_mid % 2:
            nxt = advance(cur)
            region(prev, cur, nxt, 1)
            prev, cur = cur, nxt
        slot = (n_steps - 1) % 2
        values(prev, 1 - slot)
        softmax(cur, slot, True)
        finalize_if_last(prev)
        values(cur, slot)
    finalize(n_blocks - 1)


def _attention(qa, qb, k, kc, va, vb, og, *, batch, lp, pad):
    t, hd = qa.shape
    tq = ATT_BLOCK
    n_pairs = hd // LANES
    spec = pl.BlockSpec((1, lp, LANES), lambda b, p: (b, 0, p))
    once = pl.BlockSpec((1, lp, LANES), lambda b, p: (b, 0, p), pipeline_mode=pl.Buffered(1))
    kc_spec = pl.BlockSpec((1, lp, LANES), lambda b, p: (b, 0, 0), pipeline_mode=pl.Buffered(1))
    to3 = lambda z: z.reshape(batch, lp, z.shape[-1])
    out = pl.pallas_call(
        functools.partial(_attn_kernel, pad=pad, n_blocks=lp // tq),
        grid=(batch, n_pairs),
        in_specs=[spec, spec, once, kc_spec, once, once, once],
        out_specs=spec,
        out_shape=jax.ShapeDtypeStruct((batch, lp, hd), BF16),
        scratch_shapes=[pltpu.VMEM((2, 2, tq, tq), F32), pltpu.VMEM((2, tq, 2 * tq), BF16),
                        pltpu.VMEM((2, tq, 2 * LANES), F32), pltpu.VMEM((2, tq, 1), F32),
                        pltpu.VMEM((tq, 2 * LANES), F32)],
        compiler_params=pltpu.CompilerParams(
            dimension_semantics=("parallel", "parallel"), vmem_limit_bytes=VMEM_LIMIT),
        name="fox_attention",
    )(to3(qa), to3(qb), to3(k), to3(kc), to3(va), to3(vb), to3(og))
    return out.reshape(t, hd)


def _post_ffn_kernel(h_ref, mix_ref, wo_ref, gpost_ref, gpre_ref, wup_ref, cw_ref, wdn_ref, gfpost_ref,
                     out_ref, a_scr, u_scr, act_scr, acc_scr, carry_scr, *, tiles_per_batch, pad):
    i = pl.program_id(0)
    tm = h_ref.shape[0]
    row = _row_in_batch(i, tiles_per_batch, tm)
    m = _mm(mix_ref[...], wo_ref[...])
    h1 = h_ref[...] + _rmsnorm(m, gpost_ref[...])
    out_ref[...] = h1
    a = jnp.where(row >= pad, _rmsnorm(h1, gpre_ref[...]), 0.0)
    a_scr[...] = a.astype(BF16)

    @pl.when(i % tiles_per_batch == 0)
    def _():
        carry_scr[...] = jnp.zeros_like(carry_scr)

    fc = FFN_CHUNK
    n_chunks = FFN_DIM // fc

    def project(c):
        for half, base in enumerate((0, FFN_DIM)):
            cols = slice(base + c * fc, base + (c + 1) * fc)
            u_scr[c % 2, half, :CARRY_ROWS, :] = carry_scr[:, cols]
            u_scr[c % 2, half, CARRY_ROWS:, :] = _mm(a_scr[...], wup_ref[:, cols])

    def activate(c):
        taps = [[cw_ref[s:s + 1, base + c * fc:base + (c + 1) * fc] for s in range(FFN_CONV)]
                for base in (0, FFN_DIM)]
        for r in range(0, tm, ELEM_ROWS):
            gate, up = [_causal_conv(u_scr[c % 2, half, r:r + CARRY_ROWS + ELEM_ROWS, :], taps[half])
                        for half in range(2)]
            gelu = 0.5 * gate * (1.0 + jnp.tanh(math.sqrt(2.0 / math.pi) * (gate + 0.044715 * (gate * gate * gate))))
            act_scr[c % 2, r:r + ELEM_ROWS, :] = (gelu * up).astype(BF16)
        for half, base in enumerate((0, FFN_DIM)):
            carry_scr[:, base + c * fc:base + (c + 1) * fc] = u_scr[c % 2, half, tm:, :]

    def contract(c):
        part = _mm(act_scr[c % 2], wdn_ref[c * fc:(c + 1) * fc, :])
        if c == 0:
            acc_scr[...] = part
        else:
            acc_scr[...] += part

    project(0)
    for c in range(n_chunks + 1):
        if c + 1 < n_chunks:
            project(c + 1)
        if c >= 1:
            contract(c - 1)
        if c < n_chunks:
            activate(c)
    out_ref[...] = out_ref[...] + _rmsnorm(acc_scr[...], gfpost_ref[...])


def _post_ffn(h, mix, wo, gpost, gpre, wup, cw, wdn, gfpost, *, lp, pad):
    t = h.shape[0]
    tm = ROW_TILE
    tpb = lp // tm
    row_spec = pl.BlockSpec((tm, D_MODEL), lambda i: (i, 0))
    vec = _const_spec((1, D_MODEL))
    return pl.pallas_call(
        functools.partial(_post_ffn_kernel, tiles_per_batch=tpb, pad=pad),
        grid=(t // tm,),
        in_specs=[row_spec, row_spec, _const_spec(wo.shape), vec, vec, _const_spec(wup.shape),
                  _const_spec(cw.shape), _const_spec(wdn.shape), vec],
        out_specs=row_spec,
        out_shape=jax.ShapeDtypeStruct((t, D_MODEL), F32),
        scratch_shapes=[pltpu.VMEM((tm, D_MODEL), BF16), pltpu.VMEM((2, 2, CARRY_ROWS + tm, FFN_CHUNK), F32),
                        pltpu.VMEM((2, tm, FFN_CHUNK), BF16), pltpu.VMEM((tm, D_MODEL), F32),
                        pltpu.VMEM((CARRY_ROWS, 2 * FFN_DIM), F32)],
        compiler_params=pltpu.CompilerParams(dimension_semantics=("arbitrary",), vmem_limit_bytes=VMEM_LIMIT),
        name="post_ffn",
    )(h, mix, wo, gpost, gpre, wup, cw, wdn, gfpost)


def _dn_in_kernel(h_ref, g_ref, w_ref, cw_ref, wba_ref, alog_ref, dt_ref, tri_ref,
                  q_ref, k_ref, v_ref, og_ref, bg_ref, gct_ref, a_scr, u_scr, carry_scr, *, tiles_per_batch, pad):
    i = pl.program_id(0)
    tm = h_ref.shape[0]
    row = _row_in_batch(i, tiles_per_batch, tm)
    a = jnp.where(row >= pad, _rmsnorm(h_ref[...], g_ref[...]), 0.0)
    a_scr[...] = a.astype(BF16)

    @pl.when(i % tiles_per_batch == 0)
    def _():
        carry_scr[...] = jnp.zeros_like(carry_scr)

    hd = DN_HEADS * DN_HEAD_DIM
    n_chunks = hd // PROJ_CHUNK
    heads_per_chunk = PROJ_CHUNK // DN_HEAD_DIM
    out_refs = (q_ref, k_ref, v_ref, og_ref)
    chunks = [(sec, c * PROJ_CHUNK) for c in range(n_chunks) for sec in (0, 3, 1, 2)]

    def project(idx):
        sec, lo = chunks[idx]
        cols = slice(sec * hd + lo, sec * hd + lo + PROJ_CHUNK)
        if sec < 3:
            u_scr[idx % 2, :CARRY_ROWS, :] = carry_scr[:, cols]
        u_scr[idx % 2, CARRY_ROWS:, :] = _mm(a_scr[...], w_ref[:, cols])

    def finish(idx):
        sec, lo = chunks[idx]
        if sec == 3:
            og_ref[:, lo:lo + PROJ_CHUNK] = u_scr[idx % 2, CARRY_ROWS:, :]
            return
        cols = slice(sec * hd + lo, sec * hd + lo + PROJ_CHUNK)
        taps = [cw_ref[s:s + 1, cols] for s in range(DN_CONV)]
        for r in range(0, tm, ELEM_ROWS):
            y = _causal_conv(u_scr[idx % 2, r:r + CARRY_ROWS + ELEM_ROWS, :], taps)
            y = y * _sigmoid(y)
            for hh in range(heads_per_chunk):
                yh = y[:, hh * DN_HEAD_DIM:(hh + 1) * DN_HEAD_DIM]
                if sec < 2:
                    yh = yh * lax.rsqrt(jnp.sum(yh * yh, axis=-1, keepdims=True) + EPS)
                    if sec == 0:
                        yh = yh * (DN_HEAD_DIM ** -0.5)
                col = lo + hh * DN_HEAD_DIM
                out_refs[sec][r:r + ELEM_ROWS, col:col + DN_HEAD_DIM] = yh.astype(BF16)
        carry_scr[:, cols] = u_scr[idx % 2, tm:, :]

    def decay():
        ba = _mm(a_scr[...], wba_ref[...])
        lane = lax.broadcasted_iota(jnp.int32, (1, LANES), 1)
        g = -jnp.exp(alog_ref[...]) * _softplus(ba + dt_ref[...])
        bg = jnp.where(lane < DN_HEADS, _sigmoid(ba), _sel_left(tri_ref[...], g))
        bg_ref[...] = bg
        bgt = jnp.transpose(bg)
        for c in range(tm // DN_CHUNK):
            gct_ref[c] = bgt[:2 * DN_HEADS, c * DN_CHUNK:(c + 1) * DN_CHUNK]

    project(0)
    for idx in range(len(chunks)):
        finish(idx)
        if idx + 1 < len(chunks):
            project(idx + 1)
        if idx == len(chunks) // 2:
            decay()


def _dn_in(h, g, w4, cw, wba, alog, dtb, tri, *, lp, pad):
    t = h.shape[0]
    tm = ROW_TILE
    tpb = lp // tm
    hd = DN_HEADS * DN_HEAD_DIM
    row_spec = pl.BlockSpec((tm, D_MODEL), lambda i: (i, 0))
    out_shapes = (
        jax.ShapeDtypeStruct((t, hd), BF16),
        jax.ShapeDtypeStruct((t, hd), BF16),
        jax.ShapeDtypeStruct((t, hd), BF16),
        jax.ShapeDtypeStruct((t, hd), F32),
        jax.ShapeDtypeStruct((t, LANES), F32),
        jax.ShapeDtypeStruct((t // DN_CHUNK, 2 * DN_HEADS, DN_CHUNK), F32),
    )
    consts = (g, w4, cw, wba, alog, dtb, tri)
    return pl.pallas_call(
        functools.partial(_dn_in_kernel, tiles_per_batch=tpb, pad=pad),
        grid=(t // tm,),
        in_specs=[row_spec] + [_const_spec(c.shape) for c in consts],
        out_specs=(row_spec, row_spec, row_spec, row_spec, pl.BlockSpec((tm, LANES), lambda i: (i, 0)),
                   pl.BlockSpec((tm // DN_CHUNK, 2 * DN_HEADS, DN_CHUNK), lambda i: (i, 0, 0))),
        out_shape=out_shapes,
        scratch_shapes=[pltpu.VMEM((tm, D_MODEL), BF16), pltpu.VMEM((2, CARRY_ROWS + tm, PROJ_CHUNK), F32),
                        pltpu.VMEM((CARRY_ROWS, 3 * hd), F32)],
        compiler_params=pltpu.CompilerParams(dimension_semantics=("arbitrary",), vmem_limit_bytes=VMEM_LIMIT),
        name="dn_in",
    )(h, *consts)


def _dn_kernel(q_ref, k_ref, v_ref, bg_ref, gct_ref, og_ref, gain_ref, o_ref, s_scr, *, tiles_per_batch):
    i = pl.program_id(0)
    tm = q_ref.shape[0]
    cs = DN_CHUNK
    dh = DN_HEAD_DIM
    heads = range(DN_HEADS)

    @pl.when(i % tiles_per_batch == 0)
    def _():
        s_scr[...] = jnp.zeros_like(s_scr)

    ri = lax.broadcasted_iota(jnp.int32, (cs, cs), 0)
    ci = lax.broadcasted_iota(jnp.int32, (cs, cs), 1)
    incl = ri >= ci
    strict = ri > ci
    blk = (ri // DN_BLOCK) == (ci // DN_BLOCK)

    def chunk(cidx, _):
        r0 = pl.multiple_of(cidx * cs, cs)
        rows = pl.ds(r0, cs)
        bg = bg_ref[rows, :]
        gct = gct_ref[cidx]
        cols = [slice(h * dh, (h + 1) * dh) for h in heads]
        kb = [k_ref[rows, cols[h]] for h in heads]
        qb = [q_ref[rows, cols[h]] for h in heads]
        beta = [bg[:, h:h + 1] for h in heads]
        gc = [bg[:, DN_HEADS + h:DN_HEADS + h + 1] for h in heads]
        dec = [jnp.where(incl, jnp.exp(gc[h] - gct[DN_HEADS + h:DN_HEADS + h + 1, :]), 0.0) for h in heads]
        a_mat = [jnp.where(strict, _mm_nt(kb[h], kb[h]) * dec[h] * beta[h], 0.0) for h in heads]
        off = [jnp.where(blk, 0.0, a_mat[h]) for h in heads]
        mk = [jnp.where(blk, -a_mat[h], 0.0) for h in heads]
        nk = mk
        for _ in range(3):
            mk = [_bmm(mk[h], mk[h]) for h in heads]
            nk = [nk[h] + mk[h] + _bmm(nk[h], mk[h]) for h in heads]
        b = [off[h] + _bmm(nk[h], off[h]) for h in heads]
        qn = [-b[h] for h in heads]
        bp = b
        for _ in range(2):
            bp = [_bmm(bp[h], bp[h]) for h in heads]
            qn = [qn[h] + bp[h] + _bmm(qn[h], bp[h]) for h in heads]
        nt = [qn[h] + nk[h] + _bmm(qn[h], nk[h]) for h in heads]
        egc = [jnp.exp(gc[h]) for h in heads]
        rhs = [jnp.concatenate([kb[h].astype(F32) * (beta[h] * egc[h]),
                                v_ref[rows, cols[h]].astype(F32) * beta[h]], axis=-1) for h in heads]
        wu = [rhs[h] + _bmm(nt[h], rhs[h]) for h in heads]
        qk = [jnp.where(incl, _mm_nt(qb[h], kb[h]) * dec[h], 0.0).astype(BF16) for h in heads]
        s_b = [s_scr[h].astype(BF16) for h in heads]
        u = [wu[h][:, dh:] - _bmm(wu[h][:, :dh], s_b[h]) for h in heads]
        ub = [u[h].astype(BF16) for h in heads]
        o = [_bmm(qb[h].astype(F32) * egc[h], s_b[h]) + _mm(qk[h], ub[h]) for h in heads]
        g_last = [gc[h][cs - 1:cs, :] for h in heads]
        kd = [(kb[h].astype(F32) * jnp.exp(g_last[h] - gc[h])).astype(BF16) for h in heads]
        for h in heads:
            s_scr[h] = jnp.exp(g_last[h]) * s_scr[h] + _mm_tn(kd[h], ub[h])
            og = og_ref[rows, cols[h]]
            on = _rmsnorm(o[h], gain_ref[...]) * (og * _sigmoid(og))
            o_ref[rows, cols[h]] = on.astype(o_ref.dtype)
        return 0

    lax.fori_loop(0, tm // cs, chunk, 0)


def _deltanet(q, k, v, bg, gct, og, gain, *, lp):
    t, hd = q.shape
    tm = ROW_TILE
    tpb = lp // tm
    row_spec = pl.BlockSpec((tm, hd), lambda i: (i, 0))
    return pl.pallas_call(
        functools.partial(_dn_kernel, tiles_per_batch=tpb),
        grid=(t // tm,),
        in_specs=[row_spec, row_spec, row_spec, pl.BlockSpec((tm, LANES), lambda i: (i, 0)),
                  pl.BlockSpec((tm // DN_CHUNK, 2 * DN_HEADS, DN_CHUNK), lambda i: (i, 0, 0)), row_spec,
                  _const_spec((1, DN_HEAD_DIM))],
        out_specs=row_spec,
        out_shape=jax.ShapeDtypeStruct((t, hd), BF16),
        scratch_shapes=[pltpu.VMEM((DN_HEADS, DN_HEAD_DIM, DN_HEAD_DIM), F32)],
        compiler_params=pltpu.CompilerParams(dimension_semantics=("arbitrary",), vmem_limit_bytes=VMEM_LIMIT),
        name="gated_deltanet",
    )(q, k, v, bg, gct, og, gain)


def _pad_cols(w, n):
    return jnp.pad(w, ((0, 0), (0, n - w.shape[1])))


def kernel(x, meta_tokens, norm_mix_pre, norm_mix_post, norm_ffn_pre, norm_ffn_post, attn_w_in, attn_b_forget, attn_q_norm, attn_k_norm, attn_w_out, dn_w_in, dn_conv, dn_a_log, dn_dt_bias, dn_o_norm, dn_w_out, ffn_w_up, ffn_conv, ffn_w_down):
    batch, seq, d = x.shape
    assert d == D_MODEL
    depth = norm_mix_pre.shape[0]
    length = N_META + seq
    lp = -(-length // ROW_TILE) * ROW_TILE
    pad = lp - length

    meta = jnp.broadcast_to(meta_tokens[None].astype(x.dtype), (batch, N_META, d))
    h = jnp.concatenate([jnp.zeros((batch, pad, d), x.dtype), meta, x], axis=1).reshape(batch * lp, d)

    att_hd = ATT_HEADS * ATT_HEAD_DIM
    dn_hd = DN_HEADS * DN_HEAD_DIM
    vec = lambda g: g.reshape(1, -1).astype(F32)
    iota = lambda n, ax: lax.broadcasted_iota(jnp.int32, (n, n), ax)
    head_sum = (iota(PROJ_CHUNK, 0) // ATT_HEAD_DIM == iota(PROJ_CHUNK, 1) // ATT_HEAD_DIM).astype(BF16)
    lower = iota(ROW_TILE, 0) >= iota(ROW_TILE, 1)
    tri = lower.astype(BF16)
    tri_chunk = (lower & (iota(ROW_TILE, 0) // DN_CHUNK == iota(ROW_TILE, 1) // DN_CHUNK)).astype(BF16)
    hh = lax.broadcasted_iota(jnp.int32, (SPLIT, LANES, LANES), 1)
    tgt = lax.broadcasted_iota(jnp.int32, (SPLIT, LANES, LANES), 2)
    piece = lax.broadcasted_iota(jnp.int32, (SPLIT, LANES, LANES), 0)
    place = ((hh < ATT_HEADS) & (tgt == KC_LANES * (hh // 2) + SPLIT * (hh % 2) + piece)).astype(BF16)

    for i in range(depth):
        j = i // 2
        if i % 2 == 0:
            w_in = attn_w_in[j]
            qa, qb, k, va, vb, og, kc = _attn_in(
                h, vec(norm_mix_pre[i]), w_in[:, :4 * att_hd].astype(BF16),
                _pad_cols(w_in[:, 4 * att_hd:], LANES).astype(BF16), _pad_cols(vec(attn_b_forget[j]), LANES),
                vec(jnp.tile(attn_q_norm[j], ATT_HEADS)), vec(jnp.tile(attn_k_norm[j], ATT_HEADS)),
                head_sum, tri, place, lp=lp, pad=pad)
            mix = _attention(qa, qb, k, kc, va, vb, og, batch=batch, lp=lp, pad=pad)
            w_out = attn_w_out[j]
        else:
            w_in = dn_w_in[j]
            zeros = jnp.zeros((DN_HEADS,), F32)
            alog = jnp.concatenate([zeros, dn_a_log[j]]).astype(F32)
            dtb = jnp.concatenate([zeros, dn_dt_bias[j]]).astype(F32)
            q, k, v, og, bg, gct = _dn_in(
                h, vec(norm_mix_pre[i]), w_in[:, :4 * dn_hd].astype(BF16), dn_conv[j].astype(F32),
                _pad_cols(w_in[:, 4 * dn_hd:], LANES).astype(BF16),
                _pad_cols(vec(alog), LANES), _pad_cols(vec(dtb), LANES), tri_chunk, lp=lp, pad=pad)
            mix = _deltanet(q, k, v, bg, gct, og, vec(dn_o_norm[j]), lp=lp)
            w_out = dn_w_out[j]
        h = _post_ffn(h, mix, w_out.astype(BF16), vec(norm_mix_post[i]), vec(norm_ffn_pre[i]),
                      ffn_w_up[i].astype(BF16), ffn_conv[i].astype(F32), ffn_w_down[i].astype(BF16),
                      vec(norm_ffn_post[i]), lp=lp, pad=pad)
    return h.reshape(batch, lp, d)[:, pad + N_META:]
```

```python
import functools
import math

import jax
import jax.numpy as jnp
from jax import lax
from jax.experimental import pallas as pl
from jax.experimental.pallas import tpu as pltpu

D_MODEL = 1024
N_META = 16
ATT_HEADS = 16
ATT_HEAD_DIM = 64
DN_HEADS = 8
DN_HEAD_DIM = 128
DN_CONV = 4
DN_CHUNK = 128
DN_BLOCK = 16
FFN_DIM = 2816
FFN_CONV = 3
EPS = 1e-6
NEG = -1e30
LOG2E = 1.4426950408889634

LANES = 128
ROW_TILE = 768
ATT_BLOCK = ROW_TILE
FFN_CHUNK = 256
DOWN_GROUP = 4
PROJ_CHUNK = 512
CARRY_ROWS = 8
ELEM_ROWS = 64
SOFTMAX_ROWS = 32
SPLIT = 3
KC_LANES = 8
VMEM_LIMIT = 56 * 1024 * 1024

F32 = jnp.float32
BF16 = jnp.bfloat16


def _rmsnorm(x, g):
    return x * lax.rsqrt(jnp.mean(x * x, axis=-1, keepdims=True) + EPS) * g


def _mm(a, b):
    return jnp.dot(a, b, preferred_element_type=F32)


def _mm_nt(a, b):
    return lax.dot_general(a, b, (((1,), (1,)), ((), ())), preferred_element_type=F32)


def _mm_tn(a, b):
    return lax.dot_general(a, b, (((0,), (0,)), ((), ())), preferred_element_type=F32)


def _bmm(a, b):
    return _mm(a.astype(BF16), b.astype(BF16))


def _split3(x, pieces=SPLIT):
    out = []
    for _ in range(pieces - 1):
        out.append(x.astype(BF16))
        x = x - out[-1].astype(F32)
    out.append(x.astype(BF16))
    return out


def _sel_right(x, sel, pieces=SPLIT):
    return sum(_mm(p, sel) for p in _split3(x, pieces))


def _sel_right_t(x, sel):
    return sum(_mm_nt(p, sel) for p in _split3(x))


def _sel_left(sel, x):
    return sum(_mm(sel, p) for p in _split3(x))


def _softplus(x):
    return jnp.maximum(x, 0.0) + jnp.log1p(jnp.exp(-jnp.abs(x)))


def _sigmoid(x):
    return 1.0 / (1.0 + jnp.exp(-x))


def _row_in_batch(i, tiles_per_batch, rows):
    base = (i % tiles_per_batch) * rows
    return base + lax.broadcasted_iota(jnp.int32, (rows, 1), 0)


def _causal_conv(window, taps):
    kw = len(taps)
    y = taps[kw - 1] * window
    for s in range(1, kw):
        y = y + taps[kw - 1 - s] * pltpu.roll(window, s, 0)
    return y[CARRY_ROWS:]


def _const_spec(shape):
    nd = len(shape)
    return pl.BlockSpec(shape, lambda *_: (0,) * nd, pipeline_mode=pl.Buffered(1))


def _attn_in_kernel(h_ref, g_ref, w_ref, wf_ref, bf_ref, qg_ref, kg_ref, hsum_ref, tri_ref, place_ref,
                    qa_ref, qb_ref, k_ref, va_ref, vb_ref, og_ref, kc_ref, a_scr, y_scr, carry_scr,
                    *, tiles_per_batch, pad):
    i = pl.program_id(0)
    tm = h_ref.shape[0]
    row = _row_in_batch(i, tiles_per_batch, tm)
    a = jnp.where(row >= pad, _rmsnorm(h_ref[...], g_ref[...]), 0.0)
    a_scr[...] = a.astype(BF16)

    hd = ATT_HEADS * ATT_HEAD_DIM
    n_chunks = hd // PROJ_CHUNK
    scale = ATT_HEAD_DIM ** -0.5 * LOG2E
    even = (lax.broadcasted_iota(jnp.int32, (1, PROJ_CHUNK), 1) // ATT_HEAD_DIM) % 2 == 0
    gains = (qg_ref, kg_ref, None, None)
    chunks = [(sec, c * PROJ_CHUNK) for sec in range(4) for c in range(n_chunks)]

    def project(idx):
        sec, lo = chunks[idx]
        y_scr[idx % 2] = _mm(a_scr[...], w_ref[:, sec * hd + lo: sec * hd + lo + PROJ_CHUNK])

    def finish(idx):
        sec, lo = chunks[idx]
        cols = slice(lo, lo + PROJ_CHUNK)
        y = y_scr[idx % 2]
        if gains[sec] is not None:
            ms = _sel_right(y * y, hsum_ref[...], pieces=2) * (1.0 / ATT_HEAD_DIM)
            y = y * lax.rsqrt(ms + EPS) * gains[sec][:, cols]
        if sec == 0:
            y = y * scale
            qa_ref[:, cols] = jnp.where(even, y, 0.0).astype(BF16)
            qb_ref[:, cols] = jnp.where(even, 0.0, y).astype(BF16)
        elif sec == 1:
            k_ref[:, cols] = y.astype(BF16)
        elif sec == 2:
            va_ref[:, cols] = jnp.where(even, y, 1.0).astype(BF16)
            vb_ref[:, cols] = jnp.where(even, 1.0, y).astype(BF16)
        else:
            og_ref[:, cols] = y

    project(0)
    for idx in range(len(chunks)):
        finish(idx)
        if idx + 1 < len(chunks):
            project(idx + 1)

    fl = _mm(a_scr[...], wf_ref[...]) + bf_ref[...]
    logf = -_softplus(-fl) * LOG2E

    @pl.when(i % tiles_per_batch == 0)
    def _():
        carry_scr[...] = jnp.zeros_like(carry_scr)

    c = _sel_left(tri_ref[...], logf) + carry_scr[...]
    carry_scr[...] = c[tm - 1:tm, :]
    kc = sum(_mm(p, place_ref[x]) for x, p in enumerate(_split3(-c)))
    kc_ref[...] = kc.astype(BF16)


def _attn_in(h, g, w4, wf, bfr, qg, kg, hsum, tri, place, *, lp, pad):
    t = h.shape[0]
    tm = ROW_TILE
    tpb = lp // tm
    hd = ATT_HEADS * ATT_HEAD_DIM
    row_spec = pl.BlockSpec((tm, D_MODEL), lambda i: (i, 0))
    out_shapes = (
        jax.ShapeDtypeStruct((t, hd), BF16),
        jax.ShapeDtypeStruct((t, hd), BF16),
        jax.ShapeDtypeStruct((t, hd), BF16),
        jax.ShapeDtypeStruct((t, hd), BF16),
        jax.ShapeDtypeStruct((t, hd), BF16),
        jax.ShapeDtypeStruct((t, hd), F32),
        jax.ShapeDtypeStruct((t, LANES), BF16),
    )
    consts = (g, w4, wf, bfr, qg, kg, hsum, tri, place)
    return pl.pallas_call(
        functools.partial(_attn_in_kernel, tiles_per_batch=tpb, pad=pad),
        grid=(t // tm,),
        in_specs=[row_spec] + [_const_spec(c.shape) for c in consts],
        out_specs=(row_spec,) * 6 + (pl.BlockSpec((tm, LANES), lambda i: (i, 0)),),
        out_shape=out_shapes,
        scratch_shapes=[pltpu.VMEM((tm, D_MODEL), BF16), pltpu.VMEM((2, tm, PROJ_CHUNK), F32),
                        pltpu.VMEM((1, LANES), F32)],
        compiler_params=pltpu.CompilerParams(dimension_semantics=("arbitrary",), vmem_limit_bytes=VMEM_LIMIT),
        name="attn_in",
    )(h, *consts)


def _attn_kernel(qa_ref, qb_ref, k_ref, kc_ref, va_ref, vb_ref, og_ref, o_ref,
                 s_scr, p_scr, alpha_scr, m_scr, acc_scr, *, pad, n_blocks):
    tq = ATT_BLOCK
    half = ATT_HEAD_DIM
    n_steps = n_blocks * (n_blocks + 1) // 2
    lane = lax.broadcasted_iota(jnp.int32, (tq, LANES), 1)
    first_lane = KC_LANES * pl.program_id(1)
    ones_sel = [jnp.where((lane >= first_lane + SPLIT * e) & (lane < first_lane + SPLIT * (e + 1)), 1.0, 0.0)
                .astype(BF16) for e in range(2)]
    m_scr[...] = jnp.full(m_scr.shape, NEG, F32)
    acc_scr[...] = jnp.zeros(acc_scr.shape, F32)

    def rows(b):
        return pl.ds(pl.multiple_of(b * tq, tq), tq)

    def advance(ij):
        i, j = ij
        return jnp.where(j == i, i + 1, i), jnp.where(j == i, 0, j + 1)

    def logits(ij, slot):
        i, j = ij
        k_aug = jnp.concatenate([k_ref[0, rows(j), :], kc_ref[0, rows(j), :]], axis=1)
        for e, q_ref in enumerate((qa_ref, qb_ref)):
            s_scr[slot, e] = _mm_nt(jnp.concatenate([q_ref[0, rows(i), :], ones_sel[e]], axis=1), k_aug)

    def softmax(ij, slot, masked):
        i, j = ij
        if masked:
            kpos = j * tq + lax.broadcasted_iota(jnp.int32, (1, tq), 1)
            kpos = jnp.where(kpos < pad, jnp.int32(2 ** 30), kpos)
        for r in range(0, tq, SOFTMAX_ROWS):
            rs = slice(r, r + SOFTMAX_ROWS)
            if masked:
                mask = kpos <= i * tq + r + lax.broadcasted_iota(jnp.int32, (SOFTMAX_ROWS, 1), 0)
            for e in range(2):
                s = s_scr[slot, e, rs, :]
                if masked:
                    s = jnp.where(mask, s, NEG)
                m_old = jnp.where(j == 0, NEG, m_scr[e, rs, :])
                m_new = jnp.maximum(m_old, jnp.max(s, axis=-1, keepdims=True))
                p_scr[slot, rs, e * tq:(e + 1) * tq] = jnp.exp2(s - m_new).astype(BF16)
                alpha_scr[slot, rs, e * LANES:(e + 1) * LANES] = jnp.broadcast_to(
                    jnp.exp2(m_old - m_new), (SOFTMAX_ROWS, LANES))
                m_scr[e, rs, :] = m_new

    def values(ij, slot):
        _, j = ij
        nil = jnp.zeros((tq, LANES), BF16)
        vals = jnp.concatenate([jnp.concatenate([va_ref[0, rows(j), :], nil], axis=1),
                                jnp.concatenate([nil, vb_ref[0, rows(j), :]], axis=1)], axis=0)
        acc_scr[...] = alpha_scr[slot] * acc_scr[...] + _mm(p_scr[slot], vals)

    def finalize(i):
        acc = acc_scr[...]
        outs = [acc[:, e * LANES:(e + 1) * LANES] for e in range(2)]
        outs = [x / pltpu.roll(x, half, 1) for x in outs]
        o = jnp.where(lane < half, outs[0], outs[1])
        o_ref[0, rows(i), :] = (o * _sigmoid(og_ref[0, rows(i), :])).astype(o_ref.dtype)

    def finalize_if_last(ij):
        i, j = ij

        @pl.when(j == i)
        def _():
            finalize(i)

    def region(prev, cur, nxt, slot):
        i, j = cur
        needs_mask = (j == 0) | (j == i)
        for masked, taken in ((True, needs_mask), (False, jnp.logical_not(needs_mask))):
            @pl.when(taken)
            def _():
                values(prev, 1 - slot)
                softmax(cur, slot, masked)
                logits(nxt, 1 - slot)
        finalize_if_last(prev)

    first = (jnp.int32(0), jnp.int32(0))
    logits(first, 0)
    if n_steps == 1:
        softmax(first, 0, True)
        values(first, 0)
    else:
        second = advance(first)
        softmax(first, 0, True)
        logits(second, 1)

        def two_steps(_, carry):
            prev, cur = carry
            nxt = advance(cur)
            region(prev, cur, nxt, 1)
            nxt2 = advance(nxt)
            region(cur, nxt, nxt2, 0)
            return nxt, nxt2

        n_mid = n_steps - 2
        prev, cur = lax.fori_loop(0, n_mid // 2, two_steps, (first, second))
        if n_mid % 2:
            nxt = advance(cur)
            region(prev, cur, nxt, 1)
            prev, cur = cur, nxt
        slot = (n_steps - 1) % 2
        values(prev, 1 - slot)
        softmax(cur, slot, True)
        finalize_if_last(prev)
        values(cur, slot)
    finalize(n_blocks - 1)


def _attention(qa, qb, k, kc, va, vb, og, *, batch, lp, pad):
    t, hd = qa.shape
    tq = ATT_BLOCK
    n_pairs = hd // LANES
    spec = pl.BlockSpec((1, lp, LANES), lambda b, p: (b, 0, p))
    once = pl.BlockSpec((1, lp, LANES), lambda b, p: (b, 0, p), pipeline_mode=pl.Buffered(1))
    kc_spec = pl.BlockSpec((1, lp, LANES), lambda b, p: (b, 0, 0), pipeline_mode=pl.Buffered(1))
    to3 = lambda z: z.reshape(batch, lp, z.shape[-1])
    out = pl.pallas_call(
        functools.partial(_attn_kernel, pad=pad, n_blocks=lp // tq),
        grid=(batch, n_pairs),
        in_specs=[spec, spec, once, kc_spec, once, once, once],
        out_specs=spec,
        out_shape=jax.ShapeDtypeStruct((batch, lp, hd), BF16),
        scratch_shapes=[pltpu.VMEM((2, 2, tq, tq), F32), pltpu.VMEM((2, tq, 2 * tq), BF16),
                        pltpu.VMEM((2, tq, 2 * LANES), F32), pltpu.VMEM((2, tq, 1), F32),
                        pltpu.VMEM((tq, 2 * LANES), F32)],
        compiler_params=pltpu.CompilerParams(
            dimension_semantics=("parallel", "parallel"), vmem_limit_bytes=VMEM_LIMIT),
        name="fox_attention",
    )(to3(qa), to3(qb), to3(k), to3(kc), to3(va), to3(vb), to3(og))
    return out.reshape(t, hd)


def _post_ffn_kernel(h_ref, mix_ref, wo_ref, gpost_ref, gpre_ref, wup_ref, cw_ref, wdn_ref, gfpost_ref,
                     out_ref, a_scr, u_scr, act_scr, acc_scr, carry_scr, *, tiles_per_batch, pad):
    i = pl.program_id(0)
    tm = h_ref.shape[0]
    row = _row_in_batch(i, tiles_per_batch, tm)
    m = _mm(mix_ref[...], wo_ref[...])
    h1 = h_ref[...] + _rmsnorm(m, gpost_ref[...])
    out_ref[...] = h1
    a = jnp.where(row >= pad, _rmsnorm(h1, gpre_ref[...]), 0.0)
    a_scr[...] = a.astype(BF16)

    @pl.when(i % tiles_per_batch == 0)
    def _():
        carry_scr[...] = jnp.zeros_like(carry_scr)

    fc = FFN_CHUNK
    n_chunks = FFN_DIM // fc

    def project(c):
        for half, base in enumerate((0, FFN_DIM)):
            cols = slice(base + c * fc, base + (c + 1) * fc)
            u_scr[c % 2, half, :CARRY_ROWS, :] = carry_scr[:, cols]
            u_scr[c % 2, half, CARRY_ROWS:, :] = _mm(a_scr[...], wup_ref[:, cols])

    def activate(c):
        taps = [[cw_ref[s:s + 1, base + c * fc:base + (c + 1) * fc] for s in range(FFN_CONV)]
                for base in (0, FFN_DIM)]
        for r in range(0, tm, ELEM_ROWS):
            gate, up = [_causal_conv(u_scr[c % 2, half, r:r + CARRY_ROWS + ELEM_ROWS, :], taps[half])
                        for half in range(2)]
            gelu = 0.5 * gate * (1.0 + jnp.tanh(math.sqrt(2.0 / math.pi) * (gate + 0.044715 * (gate * gate * gate))))
            act_scr[r:r + ELEM_ROWS, c * fc:(c + 1) * fc] = (gelu * up).astype(BF16)
        for half, base in enumerate((0, FFN_DIM)):
            carry_scr[:, base + c * fc:base + (c + 1) * fc] = u_scr[c % 2, half, tm:, :]

    def contract(lo, hi):
        part = _mm(act_scr[:, lo * fc:hi * fc], wdn_ref[lo * fc:hi * fc, :])
        if lo == 0:
            acc_scr[...] = part
        else:
            acc_scr[...] += part

    group_ends = list(range(DOWN_GROUP, n_chunks, DOWN_GROUP)) + [n_chunks]
    project(0)
    for c in range(n_chunks + 1):
        if c + 1 < n_chunks:
            project(c + 1)
        if c in group_ends:
            contract(c - DOWN_GROUP if c % DOWN_GROUP == 0 else c - c % DOWN_GROUP, c)
        if c < n_chunks:
            activate(c)
    out_ref[...] = out_ref[...] + _rmsnorm(acc_scr[...], gfpost_ref[...])


def _post_ffn(h, mix, wo, gpost, gpre, wup, cw, wdn, gfpost, *, lp, pad):
    t = h.shape[0]
    tm = ROW_TILE
    tpb = lp // tm
    row_spec = pl.BlockSpec((tm, D_MODEL), lambda i: (i, 0))
    vec = _const_spec((1, D_MODEL))
    return pl.pallas_call(
        functools.partial(_post_ffn_kernel, tiles_per_batch=tpb, pad=pad),
        grid=(t // tm,),
        in_specs=[row_spec, row_spec, _const_spec(wo.shape), vec, vec, _const_spec(wup.shape),
                  _const_spec(cw.shape), _const_spec(wdn.shape), vec],
        out_specs=row_spec,
        out_shape=jax.ShapeDtypeStruct((t, D_MODEL), F32),
        scratch_shapes=[pltpu.VMEM((tm, D_MODEL), BF16), pltpu.VMEM((2, 2, CARRY_ROWS + tm, FFN_CHUNK), F32),
                        pltpu.VMEM((tm, FFN_DIM), BF16), pltpu.VMEM((tm, D_MODEL), F32),
                        pltpu.VMEM((CARRY_ROWS, 2 * FFN_DIM), F32)],
        compiler_params=pltpu.CompilerParams(dimension_semantics=("arbitrary",), vmem_limit_bytes=VMEM_LIMIT),
        name="post_ffn",
    )(h, mix, wo, gpost, gpre, wup, cw, wdn, gfpost)


def _dn_in_kernel(h_ref, g_ref, w_ref, cw_ref, wba_ref, alog_ref, dt_ref, tri_ref,
                  q_ref, k_ref, v_ref, og_ref, bg_ref, gct_ref, a_scr, u_scr, carry_scr, *, tiles_per_batch, pad):
    i = pl.program_id(0)
    tm = h_ref.shape[0]
    row = _row_in_batch(i, tiles_per_batch, tm)
    a = jnp.where(row >= pad, _rmsnorm(h_ref[...], g_ref[...]), 0.0)
    a_scr[...] = a.astype(BF16)

    @pl.when(i % tiles_per_batch == 0)
    def _():
        carry_scr[...] = jnp.zeros_like(carry_scr)

    hd = DN_HEADS * DN_HEAD_DIM
    n_chunks = hd // PROJ_CHUNK
    heads_per_chunk = PROJ_CHUNK // DN_HEAD_DIM
    out_refs = (q_ref, k_ref, v_ref, og_ref)
    chunks = [(sec, c * PROJ_CHUNK) for c in range(n_chunks) for sec in (0, 3, 1, 2)]

    def project(idx):
        sec, lo = chunks[idx]
        cols = slice(sec * hd + lo, sec * hd + lo + PROJ_CHUNK)
        if sec < 3:
            u_scr[idx % 2, :CARRY_ROWS, :] = carry_scr[:, cols]
        u_scr[idx % 2, CARRY_ROWS:, :] = _mm(a_scr[...], w_ref[:, cols])

    def finish(idx):
        sec, lo = chunks[idx]
        if sec == 3:
            og_ref[:, lo:lo + PROJ_CHUNK] = u_scr[idx % 2, CARRY_ROWS:, :]
            return
        cols = slice(sec * hd + lo, sec * hd + lo + PROJ_CHUNK)
        taps = [cw_ref[s:s + 1, cols] for s in range(DN_CONV)]
        for r in range(0, tm, ELEM_ROWS):
            y = _causal_conv(u_scr[idx % 2, r:r + CARRY_ROWS + ELEM_ROWS, :], taps)
            y = y * _sigmoid(y)
            for hh in range(heads_per_chunk):
                yh = y[:, hh * DN_HEAD_DIM:(hh + 1) * DN_HEAD_DIM]
                if sec < 2:
                    yh = yh * lax.rsqrt(jnp.sum(yh * yh, axis=-1, keepdims=True) + EPS)
                    if sec == 0:
                        yh = yh * (DN_HEAD_DIM ** -0.5)
                col = lo + hh * DN_HEAD_DIM
                out_refs[sec][r:r + ELEM_ROWS, col:col + DN_HEAD_DIM] = yh.astype(BF16)
        carry_scr[:, cols] = u_scr[idx % 2, tm:, :]

    def decay():
        ba = _mm(a_scr[...], wba_ref[...])
        lane = lax.broadcasted_iota(jnp.int32, (1, LANES), 1)
        g = -jnp.exp(alog_ref[...]) * _softplus(ba + dt_ref[...])
        bg = jnp.where(lane < DN_HEADS, _sigmoid(ba), _sel_left(tri_ref[...], g))
        bg_ref[...] = bg
        bgt = jnp.transpose(bg)
        for c in range(tm // DN_CHUNK):
            gct_ref[c] = bgt[:2 * DN_HEADS, c * DN_CHUNK:(c + 1) * DN_CHUNK]

    project(0)
    for idx in range(len(chunks)):
        finish(idx)
        if idx + 1 < len(chunks):
            project(idx + 1)
        if idx == len(chunks) // 2:
            decay()


def _dn_in(h, g, w4, cw, wba, alog, dtb, tri, *, lp, pad):
    t = h.shape[0]
    tm = ROW_TILE
    tpb = lp // tm
    hd = DN_HEADS * DN_HEAD_DIM
    row_spec = pl.BlockSpec((tm, D_MODEL), lambda i: (i, 0))
    out_shapes = (
        jax.ShapeDtypeStruct((t, hd), BF16),
        jax.ShapeDtypeStruct((t, hd), BF16),
        jax.ShapeDtypeStruct((t, hd), BF16),
        jax.ShapeDtypeStruct((t, hd), F32),
        jax.ShapeDtypeStruct((t, LANES), F32),
        jax.ShapeDtypeStruct((t // DN_CHUNK, 2 * DN_HEADS, DN_CHUNK), F32),
    )
    consts = (g, w4, cw, wba, alog, dtb, tri)
    return pl.pallas_call(
        functools.partial(_dn_in_kernel, tiles_per_batch=tpb, pad=pad),
        grid=(t // tm,),
        in_specs=[row_spec] + [_const_spec(c.shape) for c in consts],
        out_specs=(row_spec, row_spec, row_spec, row_spec, pl.BlockSpec((tm, LANES), lambda i: (i, 0)),
                   pl.BlockSpec((tm // DN_CHUNK, 2 * DN_HEADS, DN_CHUNK), lambda i: (i, 0, 0))),
        out_shape=out_shapes,
        scratch_shapes=[pltpu.VMEM((tm, D_MODEL), BF16), pltpu.VMEM((2, CARRY_ROWS + tm, PROJ_CHUNK), F32),
                        pltpu.VMEM((CARRY_ROWS, 3 * hd), F32)],
        compiler_params=pltpu.CompilerParams(dimension_semantics=("arbitrary",), vmem_limit_bytes=VMEM_LIMIT),
        name="dn_in",
    )(h, *consts)


def _dn_kernel(q_ref, k_ref, v_ref, bg_ref, gct_ref, og_ref, gain_ref, o_ref, s_scr, *, tiles_per_batch):
    i = pl.program_id(0)
    tm = q_ref.shape[0]
    cs = DN_CHUNK
    dh = DN_HEAD_DIM
    heads = range(DN_HEADS)

    @pl.when(i % tiles_per_batch == 0)
    def _():
        s_scr[...] = jnp.zeros_like(s_scr)

    ri = lax.broadcasted_iota(jnp.int32, (cs, cs), 0)
    ci = lax.broadcasted_iota(jnp.int32, (cs, cs), 1)
    incl = ri >= ci
    strict = ri > ci
    blk = (ri // DN_BLOCK) == (ci // DN_BLOCK)

    def chunk(cidx, _):
        r0 = pl.multiple_of(cidx * cs, cs)
        rows = pl.ds(r0, cs)
        bg = bg_ref[rows, :]
        gct = gct_ref[cidx]
        cols = [slice(h * dh, (h + 1) * dh) for h in heads]
        kb = [k_ref[rows, cols[h]] for h in heads]
        qb = [q_ref[rows, cols[h]] for h in heads]
        beta = [bg[:, h:h + 1] for h in heads]
        gc = [bg[:, DN_HEADS + h:DN_HEADS + h + 1] for h in heads]
        dec = [jnp.where(incl, jnp.exp(gc[h] - gct[DN_HEADS + h:DN_HEADS + h + 1, :]), 0.0) for h in heads]
        a_mat = [jnp.where(strict, _mm_nt(kb[h], kb[h]) * dec[h] * beta[h], 0.0) for h in heads]
        off = [jnp.where(blk, 0.0, a_mat[h]) for h in heads]
        mk = [jnp.where(blk, -a_mat[h], 0.0) for h in heads]
        nk = mk
        for _ in range(3):
            mk = [_bmm(mk[h], mk[h]) for h in heads]
            nk = [nk[h] + mk[h] + _bmm(nk[h], mk[h]) for h in heads]
        b = [off[h] + _bmm(nk[h], off[h]) for h in heads]
        qn = [-b[h] for h in heads]
        bp = b
        for _ in range(2):
            bp = [_bmm(bp[h], bp[h]) for h in heads]
            qn = [qn[h] + bp[h] + _bmm(qn[h], bp[h]) for h in heads]
        nt = [qn[h] + nk[h] + _bmm(qn[h], nk[h]) for h in heads]
        egc = [jnp.exp(gc[h]) for h in heads]
        rhs = [jnp.concatenate([kb[h].astype(F32) * (beta[h] * egc[h]),
                                v_ref[rows, cols[h]].astype(F32) * beta[h]], axis=-1) for h in heads]
        wu = [rhs[h] + _bmm(nt[h], rhs[h]) for h in heads]
        qk = [jnp.where(incl, _mm_nt(qb[h], kb[h]) * dec[h], 0.0).astype(BF16) for h in heads]
        s_b = [s_scr[h].astype(BF16) for h in heads]
        u = [wu[h][:, dh:] - _bmm(wu[h][:, :dh], s_b[h]) for h in heads]
        ub = [u[h].astype(BF16) for h in heads]
        o = [_bmm(qb[h].astype(F32) * egc[h], s_b[h]) + _mm(qk[h], ub[h]) for h in heads]
        g_last = [gc[h][cs - 1:cs, :] for h in heads]
        kd = [(kb[h].astype(F32) * jnp.exp(g_last[h] - gc[h])).astype(BF16) for h in heads]
        for h in heads:
            s_scr[h] = jnp.exp(g_last[h]) * s_scr[h] + _mm_tn(kd[h], ub[h])
            og = og_ref[rows, cols[h]]
            on = _rmsnorm(o[h], gain_ref[...]) * (og * _sigmoid(og))
            o_ref[rows, cols[h]] = on.astype(o_ref.dtype)
        return 0

    lax.fori_loop(0, tm // cs, chunk, 0)


def _deltanet(q, k, v, bg, gct, og, gain, *, lp):
    t, hd = q.shape
    tm = ROW_TILE
    tpb = lp // tm
    row_spec = pl.BlockSpec((tm, hd), lambda i: (i, 0))
    return pl.pallas_call(
        functools.partial(_dn_kernel, tiles_per_batch=tpb),
        grid=(t // tm,),
        in_specs=[row_spec, row_spec, row_spec, pl.BlockSpec((tm, LANES), lambda i: (i, 0)),
                  pl.BlockSpec((tm // DN_CHUNK, 2 * DN_HEADS, DN_CHUNK), lambda i: (i, 0, 0)), row_spec,
                  _const_spec((1, DN_HEAD_DIM))],
        out_specs=row_spec,
        out_shape=jax.ShapeDtypeStruct((t, hd), BF16),
        scratch_shapes=[pltpu.VMEM((DN_HEADS, DN_HEAD_DIM, DN_HEAD_DIM), F32)],
        compiler_params=pltpu.CompilerParams(dimension_semantics=("arbitrary",), vmem_limit_bytes=VMEM_LIMIT),
        name="gated_deltanet",
    )(q, k, v, bg, gct, og, gain)


def _pad_cols(w, n):
    return jnp.pad(w, ((0, 0), (0, n - w.shape[1])))


def kernel(x, meta_tokens, norm_mix_pre, norm_mix_post, norm_ffn_pre, norm_ffn_post, attn_w_in, attn_b_forget, attn_q_norm, attn_k_norm, attn_w_out, dn_w_in, dn_conv, dn_a_log, dn_dt_bias, dn_o_norm, dn_w_out, ffn_w_up, ffn_conv, ffn_w_down):
    batch, seq, d = x.shape
    assert d == D_MODEL
    depth = norm_mix_pre.shape[0]
    length = N_META + seq
    lp = -(-length // ROW_TILE) * ROW_TILE
    pad = lp - length

    meta = jnp.broadcast_to(meta_tokens[None].astype(x.dtype), (batch, N_META, d))
    h = jnp.concatenate([jnp.zeros((batch, pad, d), x.dtype), meta, x], axis=1).reshape(batch * lp, d)

    att_hd = ATT_HEADS * ATT_HEAD_DIM
    dn_hd = DN_HEADS * DN_HEAD_DIM
    vec = lambda g: g.reshape(1, -1).astype(F32)
    iota = lambda n, ax: lax.broadcasted_iota(jnp.int32, (n, n), ax)
    head_sum = (iota(PROJ_CHUNK, 0) // ATT_HEAD_DIM == iota(PROJ_CHUNK, 1) // ATT_HEAD_DIM).astype(BF16)
    lower = iota(ROW_TILE, 0) >= iota(ROW_TILE, 1)
    tri = lower.astype(BF16)
    tri_chunk = (lower & (iota(ROW_TILE, 0) // DN_CHUNK == iota(ROW_TILE, 1) // DN_CHUNK)).astype(BF16)
    hh = lax.broadcasted_iota(jnp.int32, (SPLIT, LANES, LANES), 1)
    tgt = lax.broadcasted_iota(jnp.int32, (SPLIT, LANES, LANES), 2)
    piece = lax.broadcasted_iota(jnp.int32, (SPLIT, LANES, LANES), 0)
    place = ((hh < ATT_HEADS) & (tgt == KC_LANES * (hh // 2) + SPLIT * (hh % 2) + piece)).astype(BF16)

    for i in range(depth):
        j = i // 2
        if i % 2 == 0:
            w_in = attn_w_in[j]
            qa, qb, k, va, vb, og, kc = _attn_in(
                h, vec(norm_mix_pre[i]), w_in[:, :4 * att_hd].astype(BF16),
                _pad_cols(w_in[:, 4 * att_hd:], LANES).astype(BF16), _pad_cols(vec(attn_b_forget[j]), LANES),
                vec(jnp.tile(attn_q_norm[j], ATT_HEADS)), vec(jnp.tile(attn_k_norm[j], ATT_HEADS)),
                head_sum, tri, place, lp=lp, pad=pad)
            mix = _attention(qa, qb, k, kc, va, vb, og, batch=batch, lp=lp, pad=pad)
            w_out = attn_w_out[j]
        else:
            w_in = dn_w_in[j]
            zeros = jnp.zeros((DN_HEADS,), F32)
            alog = jnp.concatenate([zeros, dn_a_log[j]]).astype(F32)
            dtb = jnp.concatenate([zeros, dn_dt_bias[j]]).astype(F32)
            q, k, v, og, bg, gct = _dn_in(
                h, vec(norm_mix_pre[i]), w_in[:, :4 * dn_hd].astype(BF16), dn_conv[j].astype(F32),
                _pad_cols(w_in[:, 4 * dn_hd:], LANES).astype(BF16),
                _pad_cols(vec(alog), LANES), _pad_cols(vec(dtb), LANES), tri_chunk, lp=lp, pad=pad)
            mix = _deltanet(q, k, v, bg, gct, og, vec(dn_o_norm[j]), lp=lp)
            w_out = dn_w_out[j]
        h = _post_ffn(h, mix, w_out.astype(BF16), vec(norm_mix_post[i]), vec(norm_ffn_pre[i]),
                      ffn_w_up[i].astype(BF16), ffn_conv[i].astype(F32), ffn_w_down[i].astype(BF16),
                      vec(norm_ffn_post[i]), lp=lp, pad=pad)
    return h.reshape(batch, lp, d)[:, pad + N_META:]
```
